```python
import jax, jax.numpy as jnp
from jax import lax
import numpy as np

D_MODEL = 1024
BATCH = 8
SEQ = 8192
DEPTH = 2

CHUNK = 64
N_META = 16
A_HEADS = 8
A_HEAD_DIM = 64
A_WIDTH = A_HEADS * A_HEAD_DIM
W_LORA = 64
A_LORA = 64
G_LORA = 128
A_IN = 3 * A_WIDTH + W_LORA + A_LORA + G_LORA
A_SPLITS = [A_WIDTH, 2 * A_WIDTH, 3 * A_WIDTH, 3 * A_WIDTH + W_LORA, 3 * A_WIDTH + W_LORA + A_LORA]
LNX_EPS = 64e-5
B_WIDTH = 512
CONV_WIDTH = 31
C_WIDTH = 512
POOL_WINDOWS = (2, 4, 8, 16)
C_GROUPS = len(POOL_WINDOWS)
C_GROUP_DIM = C_WIDTH // C_GROUPS
N_BRANCH = 3
IN_WIDTH = A_IN + 2 * B_WIDTH + C_WIDTH + N_BRANCH * D_MODEL
IN_SPLITS = [A_IN, A_IN + 2 * B_WIDTH, A_IN + 2 * B_WIDTH + C_WIDTH]
D_FF = 2816
N_EXPERTS = 8
TOP_K = 2
MOE_BLOCK = 256
N_DENSE = (DEPTH + 1) // 2
N_MOE = DEPTH // 2
RMS_EPS = 1e-6
LN_EPS = 1e-5

kernel_name = 'hybrid_stream_encoder_block'


def rmsnorm(x, g):
    xf = x.astype(jnp.float32)
    y = xf * lax.rsqrt(jnp.mean(xf * xf, -1, keepdims=True) + RMS_EPS)
    return (y * g.astype(jnp.float32)).astype(x.dtype)


def shift_one(z):
    return jnp.pad(z, ((0, 0), (1, 0), (0, 0)))[:, :-1]


def _wkv_step(S, inp):
    r, dec, k, v, a_vec, b_vec = inp
    Sa = jnp.einsum('bhvk,bhk->bhv', S, a_vec)
    S = S * dec[:, :, None, :] + Sa[..., None] * b_vec[:, :, None, :] + v[..., None] * k[:, :, None, :]
    return S, jnp.einsum('bhvk,bhk->bhv', S, r)


def _wkv_scan(S, seq):
    return lax.scan(_wkv_step, S, seq)


def wkv7(r, dec, k, v, a_vec, b_vec):
    seq = tuple(jnp.moveaxis(t.astype(jnp.float32), 1, 0) for t in (r, dec, k, v, a_vec, b_vec))
    bn, nh, nd = r.shape[0], r.shape[2], r.shape[3]
    S0 = jnp.zeros((bn, nh, nd, nd), jnp.float32)
    S1, y_meta = _wkv_scan(S0, tuple(t[:N_META] for t in seq))
    frames = tuple(t[N_META:].reshape((-1, CHUNK) + t.shape[1:]) for t in seq)
    _, y_frames = lax.scan(_wkv_scan, S1, frames)
    y = jnp.concatenate([y_meta, y_frames.reshape((-1,) + y_meta.shape[1:])], 0)
    return jnp.moveaxis(y, 0, 1)


def rwkv7_mixer(z, mu, w0, w2, a0, a2, g2, k_k, k_a, r_k, lnx_w, lnx_b):
    bn, L, _ = z.shape
    z = z + (shift_one(z) - z) * mu
    r, k, v, wd, ad, gd = jnp.split(z, A_SPLITS, -1)
    w_log = -jax.nn.softplus(-(w0 + jnp.tanh(wd) @ w2)) - 0.5
    dec = jnp.exp(-jnp.exp(w_log.astype(jnp.float32)))
    a = jax.nn.sigmoid(a0 + ad @ a2)
    g = jax.nn.sigmoid(gd) @ g2
    hs = lambda t: t.reshape(bn, L, A_HEADS, A_HEAD_DIM)
    kk = hs(k * k_k).astype(jnp.float32)
    kk = kk / jnp.maximum(jnp.sqrt(jnp.sum(kk * kk, -1, keepdims=True)), 1e-12)
    k = k * (1.0 + (a - 1.0) * k_a)
    r_h, k_h, v_h, a_h = hs(r), hs(k), hs(v), hs(a)
    y = wkv7(r_h, hs(dec), k_h, v_h, -kk, kk * a_h.astype(jnp.float32))
    mean = jnp.mean(y, -1, keepdims=True)
    var = jnp.mean(jnp.square(y - mean), -1, keepdims=True)
    y = ((y - mean) * lax.rsqrt(var + LNX_EPS)).reshape(bn, L, A_WIDTH)
    y = y * lnx_w.astype(jnp.float32) + lnx_b.astype(jnp.float32)
    bonus = jnp.sum((r_h * k_h * r_k).astype(jnp.float32), -1, keepdims=True) * v_h.astype(jnp.float32)
    y = (y + bonus.reshape(bn, L, A_WIDTH)) * g.astype(jnp.float32)
    return y.astype(z.dtype)


def conformer_conv(z, conv_w, conv_b, ln_g, ln_b):
    c = z[..., :B_WIDTH] * jax.nn.sigmoid(z[..., B_WIDTH:])
    c = lax.conv_general_dilated(c, conv_w.astype(c.dtype), window_strides=(1,),
                                 padding=[(CONV_WIDTH - 1, 0)],
                                 dimension_numbers=('NWC', 'WIO', 'NWC'),
                                 feature_group_count=B_WIDTH) + conv_b
    cf = c.astype(jnp.float32)
    mean = jnp.mean(cf, -1, keepdims=True)
    var = jnp.mean(jnp.square(cf - mean), -1, keepdims=True)
    cf = (cf - mean) * lax.rsqrt(var + LN_EPS) * ln_g.astype(jnp.float32) + ln_b.astype(jnp.float32)
    return jax.nn.silu(cf).astype(z.dtype)


def multiscale_pool(z, pool_w, pool_scale):
    bn, L, _ = z.shape
    zg = z.reshape(bn, L, C_GROUPS, C_GROUP_DIM).astype(jnp.float32)
    cs = jnp.cumsum(zg, axis=1)
    pos = jnp.arange(1, L + 1, dtype=jnp.float32)
    means = []
    for j, w in enumerate(POOL_WINDOWS):
        c = cs[:, :, j]
        prev = jnp.pad(c, ((0, 0), (w, 0), (0, 0)))[:, :L]
        means.append((c - prev) / jnp.minimum(pos, w)[None, :, None])
    pooled = (jnp.stack(means, 2) - zg).astype(z.dtype)
    y = jnp.einsum('blgc,gcd->blgd', pooled, pool_w).reshape(bn, L, C_WIDTH)
    return y * pool_scale


def swiglu(h, w_gate, w_up, w_down):
    return (jax.nn.silu(h @ w_gate) * (h @ w_up)) @ w_down


def moe_swiglu(h, router_w, router_b, e_gate, e_up, e_down):
    bn, L, d = h.shape
    t = h.reshape(-1, d)
    T = t.shape[0]
    logits = (t @ router_w).astype(jnp.float32) + router_b.astype(jnp.float32)
    top_l, top_e = lax.top_k(logits, TOP_K)
    gates = jax.nn.softmax(top_l, -1)
    flat_e = top_e.reshape(-1)
    flat_tok = jnp.repeat(jnp.arange(T, dtype=jnp.int32), TOP_K)
    order = jnp.argsort(flat_e)
    se, stok, sg = flat_e[order], flat_tok[order], gates.reshape(-1)[order]
    counts = jnp.bincount(flat_e, length=N_EXPERTS)
    padded = (counts + MOE_BLOCK - 1) // MOE_BLOCK * MOE_BLOCK
    start_sorted = jnp.cumsum(counts) - counts
    pad_end = jnp.cumsum(padded)
    start_pad = pad_end - padded
    dest = start_pad[se] + jnp.arange(T * TOP_K, dtype=jnp.int32) - start_sorted[se]
    n_blocks = (T * TOP_K + N_EXPERTS * (MOE_BLOCK - 1) + MOE_BLOCK - 1) // MOE_BLOCK
    slot_tok = jnp.full((n_blocks * MOE_BLOCK,), T, jnp.int32).at[dest].set(stok)
    block_e = jnp.minimum(jnp.searchsorted(pad_end, jnp.arange(n_blocks) * MOE_BLOCK, side='right'), N_EXPERTS - 1)
    t_pad = jnp.concatenate([t, jnp.zeros((1, d), t.dtype)], 0)

    def run_block(args):
        tok, e = args
        xb = t_pad[tok]
        return swiglu(xb, e_gate[e], e_up[e], e_down[e])

    y_slots = lax.map(run_block, (slot_tok.reshape(n_blocks, MOE_BLOCK), block_e)).reshape(-1, d)
    y = jax.ops.segment_sum(y_slots[dest] * sg[:, None].astype(y_slots.dtype), stok, num_segments=T)
    return y.reshape(bn, L, d)


def setup_inputs(seed: int = 0) -> dict:
    key = jax.random.key(seed)
    ks = iter(jax.random.split(key, 48))
    f32 = jnp.float32
    nrm = lambda shape, scale: scale * jax.random.normal(next(ks), shape, f32)
    return {
        'x': nrm((BATCH, SEQ, D_MODEL), 1.0),
        'meta_tokens': nrm((N_META, D_MODEL), 1.0),
        'mix_norm_g': 1.0 + nrm((DEPTH, D_MODEL), 0.05),
        'w_in': nrm((DEPTH, D_MODEL, IN_WIDTH), D_MODEL ** -0.5),
        'gate_b': nrm((DEPTH, N_BRANCH * D_MODEL), 0.1),
        'a_mu': jax.random.uniform(next(ks), (DEPTH, A_IN), f32),
        'a_w0': -2.0 + nrm((DEPTH, A_WIDTH), 1.0),
        'a_w2': nrm((DEPTH, W_LORA, A_WIDTH), 0.5 * W_LORA ** -0.5),
        'a_a0': nrm((DEPTH, A_WIDTH), 0.5),
        'a_a2': nrm((DEPTH, A_LORA, A_WIDTH), 0.5 * A_LORA ** -0.5),
        'a_g2': nrm((DEPTH, G_LORA, A_WIDTH), G_LORA ** -0.5),
        'a_kk': 0.85 + nrm((DEPTH, A_WIDTH), 0.05),
        'a_ka': 1.0 + nrm((DEPTH, A_WIDTH), 0.05),
        'a_rk': nrm((DEPTH, A_HEADS, A_HEAD_DIM), 0.1),
        'a_lnx_w': 1.0 + nrm((DEPTH, A_WIDTH), 0.05),
        'a_lnx_b': nrm((DEPTH, A_WIDTH), 0.01),
        'b_conv_w': nrm((DEPTH, CONV_WIDTH, 1, B_WIDTH), CONV_WIDTH ** -0.5),
        'b_conv_b': nrm((DEPTH, B_WIDTH), 0.01),
        'b_ln_g': 1.0 + nrm((DEPTH, B_WIDTH), 0.05),
        'b_ln_b': nrm((DEPTH, B_WIDTH), 0.01),
        'c_pool_w': nrm((DEPTH, C_GROUPS, C_GROUP_DIM, C_GROUP_DIM), C_GROUP_DIM ** -0.5),
        'c_scale': 1.0 + nrm((DEPTH, C_WIDTH), 0.05),
        'proj_a': nrm((DEPTH, A_WIDTH, D_MODEL), A_WIDTH ** -0.5),
        'proj_b': nrm((DEPTH, B_WIDTH, D_MODEL), B_WIDTH ** -0.5),
        'proj_c': nrm((DEPTH, C_WIDTH, D_MODEL), C_WIDTH ** -0.5),
        'w_out': nrm((DEPTH, D_MODEL, D_MODEL), 0.5 * D_MODEL ** -0.5),
        'ffn_norm_g': 1.0 + nrm((DEPTH, D_MODEL), 0.05),
        'ffn_gate': nrm((N_DENSE, D_MODEL, D_FF), D_MODEL ** -0.5),
        'ffn_up': nrm((N_DENSE, D_MODEL, D_FF), D_MODEL ** -0.5),
        'ffn_down': nrm((N_DENSE, D_FF, D_MODEL), 0.5 * D_FF ** -0.5),
        'router_w': nrm((N_MOE, D_MODEL, N_EXPERTS), D_MODEL ** -0.5),
        'router_b': nrm((N_MOE, N_EXPERTS), 0.01),
        'exp_gate': nrm((N_MOE, N_EXPERTS, D_MODEL, D_FF), D_MODEL ** -0.5),
        'exp_up': nrm((N_MOE, N_EXPERTS, D_MODEL, D_FF), D_MODEL ** -0.5),
        'exp_down': nrm((N_MOE, N_EXPERTS, D_FF, D_MODEL), 0.5 * D_FF ** -0.5),
        'final_norm_g': 1.0 + nrm((D_MODEL,), 0.05),
    }


def reference(x, meta_tokens, mix_norm_g, w_in, gate_b, a_mu, a_w0, a_w2, a_a0, a_a2, a_g2,
              a_kk, a_ka, a_rk, a_lnx_w, a_lnx_b, b_conv_w, b_conv_b, b_ln_g, b_ln_b,
              c_pool_w, c_scale, proj_a, proj_b, proj_c, w_out, ffn_norm_g, ffn_gate, ffn_up,
              ffn_down, router_w, router_b, exp_gate, exp_up, exp_down, final_norm_g):
    bn = x.shape[0]
    meta = jnp.broadcast_to(meta_tokens[None].astype(x.dtype), (bn, N_META, D_MODEL))
    h = jnp.concatenate([meta, x], 1)
    for i in range(DEPTH):
        n = rmsnorm(h, mix_norm_g[i])
        u = n @ w_in[i]
        z_a, z_b, z_c, z_g = jnp.split(u, IN_SPLITS, -1)
        y_a = rwkv7_mixer(z_a, a_mu[i], a_w0[i], a_w2[i], a_a0[i], a_a2[i], a_g2[i],
                          a_kk[i], a_ka[i], a_rk[i], a_lnx_w[i], a_lnx_b[i])
        y_b = conformer_conv(z_b, b_conv_w[i], b_conv_b[i], b_ln_g[i], b_ln_b[i])
        y_c = multiscale_pool(z_c, c_pool_w[i], c_scale[i])
        g_a, g_b, g_c = jnp.split(jax.nn.sigmoid(z_g + gate_b[i]), N_BRANCH, -1)
        merged = g_a * (y_a @ proj_a[i]) + g_b * (y_b @ proj_b[i]) + g_c * (y_c @ proj_c[i])
        h = h + merged @ w_out[i]
        n = rmsnorm(h, ffn_norm_g[i])
        if i % 2 == 0:
            f = swiglu(n, ffn_gate[i // 2], ffn_up[i // 2], ffn_down[i // 2])
        else:
            f = moe_swiglu(n, router_w[i // 2], router_b[i // 2], exp_gate[i // 2], exp_up[i // 2], exp_down[i // 2])
        h = h + f
    return rmsnorm(h, final_norm_g)[:, N_META:]
```

```python
import functools

import jax
import jax.numpy as jnp
from jax import lax
from jax.experimental import pallas as pl
from jax.experimental.pallas import tpu as pltpu

F32 = jnp.float32
BF16 = jnp.bfloat16

N_META = 16
A_HEADS = 8
A_HEAD_DIM = 64
A_WIDTH = A_HEADS * A_HEAD_DIM
W_LORA = 64
A_LORA = 64
G_LORA = 128
A_IN = 3 * A_WIDTH + W_LORA + A_LORA + G_LORA
LNX_EPS = 64e-5
B_WIDTH = 512
CONV_WIDTH = 31
C_WIDTH = 512
POOL_WINDOWS = (2, 4, 8, 16)
C_GROUP_DIM = C_WIDTH // len(POOL_WINDOWS)
N_EXPERTS = 8
TOP_K = 2
RMS_EPS = 1e-6
LN_EPS = 1e-5

V7X_LANES = 128
V7X_SUBLANES = 8
VMEM_LIMIT_BYTES = 56 * 1024 * 1024
ROUTE_LANES = 128
MOE_ROWS = 512
CONV_HIST = 32
POOL_HIST = 16


def _pick(n, candidates):
    for c in candidates:
        if n % c == 0:
            return c
    raise ValueError(f"no tile in {candidates} divides {n}")


def _params(*sem):
    return pltpu.CompilerParams(dimension_semantics=sem, vmem_limit_bytes=VMEM_LIMIT_BYTES)


def _rms(x, g):
    return x * lax.rsqrt(jnp.mean(x * x, -1, keepdims=True) + RMS_EPS) * g


def _silu(x):
    return x * jax.nn.sigmoid(x)


def _norm_mm_kernel(x_ref, g_ref, w_ref, o_ref, xn_ref):
    @pl.when(pl.program_id(1) == 0)
    def _():
        xn_ref[...] = _rms(x_ref[...], g_ref[...]).astype(BF16)

    o_ref[...] = jnp.dot(xn_ref[...], w_ref[...], preferred_element_type=F32)


def norm_matmul(x, g, w):
    m, k = x.shape
    n = w.shape[1]
    tm = _pick(m, (1152, 576, 384, 128, 8))
    tn = _pick(n, (1024, 896, 768, 512, 256, 128))
    return pl.pallas_call(
        _norm_mm_kernel,
        grid=(m // tm, n // tn),
        in_specs=[
            pl.BlockSpec((tm, k), lambda i, j: (i, 0)),
            pl.BlockSpec((1, k), lambda i, j: (0, 0)),
            pl.BlockSpec((k, tn), lambda i, j: (0, j)),
        ],
        out_specs=pl.BlockSpec((tm, tn), lambda i, j: (i, j)),
        out_shape=jax.ShapeDtypeStruct((m, n), F32),
        scratch_shapes=[pltpu.VMEM((tm, k), BF16)],
        compiler_params=_params("parallel", "arbitrary"),
    )(x, g.reshape(1, k), w)


def _act_mm_kernel(x_ref, w_ref, o_ref, *, act):
    x = x_ref[...]
    if act == "tanh":
        x = jnp.tanh(x)
    elif act == "sigmoid":
        x = jax.nn.sigmoid(x)
    o_ref[...] = jnp.dot(x.astype(BF16), w_ref[...], preferred_element_type=F32)


def act_matmul(x, w, act):
    m, k = x.shape
    n = w.shape[1]
    tm = _pick(m, (1152, 576, 384, 128, 8))
    return pl.pallas_call(
        functools.partial(_act_mm_kernel, act=act),
        grid=(m // tm,),
        in_specs=[pl.BlockSpec((tm, k), lambda i: (i, 0)), pl.BlockSpec((k, n), lambda i: (0, 0))],
        out_specs=pl.BlockSpec((tm, n), lambda i: (i, 0)),
        out_shape=jax.ShapeDtypeStruct((m, n), F32),
        compiler_params=_params("parallel"),
    )(x, w)


def _sublane_total(x):
    p = jnp.sum(x, axis=0)
    p = p + pltpu.roll(p, 4, 0)
    p = p + pltpu.roll(p, 2, 0)
    return p + pltpu.roll(p, 1, 0)


def _wkv_kernel(a_ref, w_ref, b_ref, k_ref, r_ref, v_ref, y_ref, s_ref, *, tt, nv):
    @pl.when(pl.program_id(0) == 0)
    def _():
        s_ref[...] = jnp.zeros_like(s_ref)

    def step(t, carry):
        a = a_ref[t]
        w = w_ref[t]
        b = b_ref[t]
        k = k_ref[t]
        r = r_ref[t]
        wr = w * r
        br = _sublane_total(b * r)
        kr = _sublane_total(k * r)
        for vi in range(nv):
            s = s_ref[vi]
            vrow = v_ref[t, pl.ds(vi, 1), :]
            sa = _sublane_total(s * a)
            y0 = _sublane_total(s * wr)
            s_ref[vi] = s * w + sa[None] * b + vrow[None] * k
            y = y0 + sa * br + vrow * kr
            y_ref[t, pl.ds(vi, 1), :] = y[0:1]
        return carry

    lax.fori_loop(0, tt, step, 0)


def wkv_scan(a_s, w_s, b_s, k_s, r_s, v_s):
    L = a_s.shape[0]
    nv = v_s.shape[1]
    tt = _pick(L, (48, 16, 8, 1))
    vec = pl.BlockSpec((tt, 8, 8, V7X_LANES), lambda i: (i, 0, 0, 0))
    row = pl.BlockSpec((tt, nv, V7X_LANES), lambda i: (i, 0, 0))
    return pl.pallas_call(
        functools.partial(_wkv_kernel, tt=tt, nv=nv),
        grid=(L // tt,),
        in_specs=[vec, vec, vec, vec, vec, row],
        out_specs=row,
        out_shape=jax.ShapeDtypeStruct((L, nv, V7X_LANES), F32),
        scratch_shapes=[pltpu.VMEM((nv, 8, 8, V7X_LANES), F32)],
        compiler_params=_params("arbitrary"),
    )(a_s, w_s, b_s, k_s, r_s, v_s)


def _conv_kernel(z_ref, cw_ref, cb_ref, lg_ref, lb_ref, o_ref, buf_ref, *, tt):
    @pl.when(pl.program_id(1) == 0)
    def _():
        buf_ref[pl.ds(0, CONV_HIST), :] = jnp.zeros((CONV_HIST, B_WIDTH), F32)

    z = z_ref[0]
    buf_ref[pl.ds(CONV_HIST, tt), :] = z[:, :B_WIDTH] * jax.nn.sigmoid(z[:, B_WIDTH:])
    acc = jnp.zeros((tt, B_WIDTH), F32) + cb_ref[...]
    for j in range(CONV_WIDTH):
        acc = acc + cw_ref[pl.ds(j, 1), :] * buf_ref[pl.ds(CONV_HIST - (CONV_WIDTH - 1) + j, tt), :]
    mean = jnp.mean(acc, -1, keepdims=True)
    cen = acc - mean
    var = jnp.mean(cen * cen, -1, keepdims=True)
    o_ref[0] = _silu(cen * lax.rsqrt(var + LN_EPS) * lg_ref[...] + lb_ref[...])
    buf_ref[pl.ds(0, CONV_HIST), :] = buf_ref[pl.ds(tt, CONV_HIST), :]


def conformer_conv(z_b, conv_w, conv_b, ln_g, ln_b):
    bn, L, _ = z_b.shape
    tt = _pick(L, (912, 456, 304, 144, 48))
    row = lambda a: a.reshape(1, B_WIDTH)
    const = lambda shape: pl.BlockSpec(shape, lambda b, i: (0, 0))
    return pl.pallas_call(
        functools.partial(_conv_kernel, tt=tt),
        grid=(bn, L // tt),
        in_specs=[
            pl.BlockSpec((1, tt, 2 * B_WIDTH), lambda b, i: (b, i, 0)),
            const((CONV_WIDTH, B_WIDTH)),
            const((1, B_WIDTH)),
            const((1, B_WIDTH)),
            const((1, B_WIDTH)),
        ],
        out_specs=pl.BlockSpec((1, tt, B_WIDTH), lambda b, i: (b, i, 0)),
        out_shape=jax.ShapeDtypeStruct((bn, L, B_WIDTH), F32),
        scratch_shapes=[pltpu.VMEM((tt + CONV_HIST, B_WIDTH), F32)],
        compiler_params=_params("parallel", "arbitrary"),
    )(z_b, conv_w.reshape(CONV_WIDTH, B_WIDTH), row(conv_b), row(ln_g), row(ln_b))


def _pool_kernel(z_ref, pw_ref, sc_ref, o_ref, buf_ref, *, tt):
    i = pl.program_id(1)

    @pl.when(i == 0)
    def _():
        buf_ref[pl.ds(0, POOL_HIST), :] = jnp.zeros((POOL_HIST, C_WIDTH), F32)

    buf_ref[pl.ds(POOL_HIST, tt), :] = z_ref[0]
    pos = (lax.broadcasted_iota(jnp.int32, (tt, C_GROUP_DIM), 0) + (i * tt + 1)).astype(F32)
    for g, win in enumerate(POOL_WINDOWS):
        lanes = pl.ds(g * C_GROUP_DIM, C_GROUP_DIM)
        cur = buf_ref[pl.ds(POOL_HIST, tt), lanes]
        tot = cur
        for d in range(1, win):
            tot = tot + buf_ref[pl.ds(POOL_HIST - d, tt), lanes]
        pooled = tot / jnp.minimum(pos, float(win)) - cur
        y = jnp.dot(pooled.astype(BF16), pw_ref[g], preferred_element_type=F32)
        o_ref[0, :, lanes] = y * sc_ref[:, lanes]
    buf_ref[pl.ds(0, POOL_HIST), :] = buf_ref[pl.ds(tt, POOL_HIST), :]


def multiscale_pool(z_c, pool_w, pool_scale):
    bn, L, _ = z_c.shape
    tt = _pick(L, (912, 456, 304, 144, 48))
    return pl.pallas_call(
        functools.partial(_pool_kernel, tt=tt),
        grid=(bn, L // tt),
        in_specs=[
            pl.BlockSpec((1, tt, C_WIDTH), lambda b, i: (b, i, 0)),
            pl.BlockSpec(pool_w.shape, lambda b, i: (0, 0, 0)),
            pl.BlockSpec((1, C_WIDTH), lambda b, i: (0, 0)),
        ],
        out_specs=pl.BlockSpec((1, tt, C_WIDTH), lambda b, i: (b, i, 0)),
        out_shape=jax.ShapeDtypeStruct((bn, L, C_WIDTH), F32),
        scratch_shapes=[pltpu.VMEM((tt + POOL_HIST, C_WIDTH), F32)],
        compiler_params=_params("parallel", "arbitrary"),
    )(z_c, pool_w, pool_scale.reshape(1, C_WIDTH))


def _merge_kernel(ya_ref, yb_ref, yc_ref, zg_ref, gb_ref, h_ref, pa_ref, pb_ref, pc_ref, wo_ref, o_ref):
    d = h_ref.shape[-1]
    gates = jax.nn.sigmoid(zg_ref[...] + gb_ref[...])
    proj = lambda y_ref, p_ref: jnp.dot(y_ref[...].astype(BF16), p_ref[...], preferred_element_type=F32)
    merged = (gates[:, :d] * proj(ya_ref, pa_ref) + gates[:, d:2 * d] * proj(yb_ref, pb_ref)
              + gates[:, 2 * d:] * proj(yc_ref, pc_ref))
    o_ref[...] = h_ref[...] + jnp.dot(merged.astype(BF16), wo_ref[...], preferred_element_type=F32)


def merge(y_a, y_b, y_c, z_g, gate_b, h, proj_a, proj_b, proj_c, w_out):
    m, d = h.shape
    tm = _pick(m, (384, 128, 8))
    rows = lambda width: pl.BlockSpec((tm, width), lambda i: (i, 0))
    const = lambda a: pl.BlockSpec(a.shape, lambda i: (0, 0))
    gate_b = gate_b.reshape(1, -1)
    return pl.pallas_call(
        _merge_kernel,
        grid=(m // tm,),
        in_specs=[rows(A_WIDTH), rows(B_WIDTH), rows(C_WIDTH), rows(3 * d), const(gate_b), rows(d),
                  const(proj_a), const(proj_b), const(proj_c), const(w_out)],
        out_specs=rows(d),
        out_shape=jax.ShapeDtypeStruct((m, d), F32),
        compiler_params=_params("parallel"),
    )(y_a, y_b, y_c, z_g, gate_b, h, proj_a, proj_b, proj_c, w_out)


def _ffn_kernel(h_ref, g_ref, fg_ref, wg_ref, wu_ref, wd_ref, o_ref, xn_ref, acc_ref, *, final):
    j = pl.program_id(1)

    @pl.when(j == 0)
    def _():
        xn_ref[...] = _rms(h_ref[...], g_ref[...]).astype(BF16)
        acc_ref[...] = jnp.zeros_like(acc_ref)

    x = xn_ref[...]
    gate = jnp.dot(x, wg_ref[...], preferred_element_type=F32)
    up = jnp.dot(x, wu_ref[...], preferred_element_type=F32)
    acc_ref[...] += jnp.dot((_silu(gate) * up).astype(BF16), wd_ref[...], preferred_element_type=F32)

    @pl.when(j == pl.num_programs(1) - 1)
    def _():
        hh = h_ref[...] + acc_ref[...]
        o_ref[...] = _rms(hh, fg_ref[...]) if final else hh


def dense_ffn(h, g, w_gate, w_up, w_down, final_g, final):
    m, d = h.shape
    ff = w_gate.shape[1]
    tm = _pick(m, (1152, 576, 384, 128, 8))
    tf = _pick(ff, (256, 128))
    return pl.pallas_call(
        functools.partial(_ffn_kernel, final=final),
        grid=(m // tm, ff // tf),
        in_specs=[
            pl.BlockSpec((tm, d), lambda i, j: (i, 0)),
            pl.BlockSpec((1, d), lambda i, j: (0, 0)),
            pl.BlockSpec((1, d), lambda i, j: (0, 0)),
            pl.BlockSpec((d, tf), lambda i, j: (0, j)),
            pl.BlockSpec((d, tf), lambda i, j: (0, j)),
            pl.BlockSpec((tf, d), lambda i, j: (j, 0)),
        ],
        out_specs=pl.BlockSpec((tm, d), lambda i, j: (i, 0)),
        out_shape=jax.ShapeDtypeStruct((m, d), F32),
        scratch_shapes=[pltpu.VMEM((tm, d), BF16), pltpu.VMEM((tm, d), F32)],
        compiler_params=_params("parallel", "arbitrary"),
    )(h, g.reshape(1, d), final_g.reshape(1, d), w_gate, w_up, w_down)


def _router_kernel(h_ref, g_ref, rw_ref, rb_ref, n_ref, l_ref):
    n = _rms(h_ref[...], g_ref[...])
    n_ref[...] = n
    l_ref[...] = jnp.dot(n, rw_ref[...], preferred_element_type=F32,
                         precision=lax.Precision.HIGHEST) + rb_ref[...]


def router(h, g, router_w, router_b):
    m, d = h.shape
    tm = _pick(m, (384, 128, 8))
    rw = jnp.zeros((d, ROUTE_LANES), F32).at[:, :N_EXPERTS].set(router_w)
    rb = jnp.zeros((1, ROUTE_LANES), F32).at[0, :N_EXPERTS].set(router_b)
    return pl.pallas_call(
        _router_kernel,
        grid=(m // tm,),
        in_specs=[pl.BlockSpec((tm, d), lambda i: (i, 0)), pl.BlockSpec((1, d), lambda i: (0, 0)),
                  pl.BlockSpec((d, ROUTE_LANES), lambda i: (0, 0)), pl.BlockSpec((1, ROUTE_LANES), lambda i: (0, 0))],
        out_specs=[pl.BlockSpec((tm, d), lambda i: (i, 0)), pl.BlockSpec((tm, ROUTE_LANES), lambda i: (i, 0))],
        out_shape=[jax.ShapeDtypeStruct((m, d), F32), jax.ShapeDtypeStruct((m, ROUTE_LANES), F32)],
        compiler_params=_params("parallel"),
    )(h, g.reshape(1, d), rw, rb)


def _gather_rows(idx_ref, src_hbm, dst_ref, sem, n_rows):
    def row_copy(r):
        return pltpu.make_async_copy(src_hbm.at[pl.ds(idx_ref[0, 0, r], 1)], dst_ref.at[pl.ds(r, 1)], sem)

    def start(r, c):
        row_copy(r).start()
        return c

    def wait(r, c):
        row_copy(r).wait()
        return c

    lax.fori_loop(0, n_rows, start, 0)
    lax.fori_loop(0, n_rows, wait, 0)


def _moe_ffn_kernel(be_ref, tok_ref, sg_ref, x_hbm, wg_ref, wu_ref, wd_ref, o_ref, xbuf_ref, acc_ref, sem):
    del be_ref
    j = pl.program_id(1)

    @pl.when(j == 0)
    def _():
        _gather_rows(tok_ref, x_hbm, xbuf_ref, sem, xbuf_ref.shape[0])
        acc_ref[...] = jnp.zeros_like(acc_ref)

    x = xbuf_ref[...].astype(BF16)
    gate = jnp.dot(x, wg_ref[0], preferred_element_type=F32)
    up = jnp.dot(x, wu_ref[0], preferred_element_type=F32)
    acc_ref[...] += jnp.dot((_silu(gate) * up).astype(BF16), wd_ref[0], preferred_element_type=F32)

    @pl.when(j == pl.num_programs(1) - 1)
    def _():
        o_ref[...] = acc_ref[...] * sg_ref[...]


def moe_expert_ffn(n, block_e, slot_tok, slot_gate, e_gate, e_up, e_down):
    d = n.shape[1]
    ff = e_gate.shape[2]
    nb = block_e.shape[0]
    tf = _pick(ff, (256, 128))
    grid_spec = pltpu.PrefetchScalarGridSpec(
        num_scalar_prefetch=1,
        grid=(nb, ff // tf),
        in_specs=[
            pl.BlockSpec((1, 1, MOE_ROWS), lambda i, j, be: (i, 0, 0), memory_space=pltpu.SMEM),
            pl.BlockSpec((MOE_ROWS, 1), lambda i, j, be: (i, 0)),
            pl.BlockSpec(memory_space=pl.ANY),
            pl.BlockSpec((1, d, tf), lambda i, j, be: (be[i], 0, j)),
            pl.BlockSpec((1, d, tf), lambda i, j, be: (be[i], 0, j)),
            pl.BlockSpec((1, tf, d), lambda i, j, be: (be[i], j, 0)),
        ],
        out_specs=pl.BlockSpec((MOE_ROWS, d), lambda i, j, be: (i, 0)),
        scratch_shapes=[pltpu.VMEM((MOE_ROWS, d), F32), pltpu.VMEM((MOE_ROWS, d), F32),
                        pltpu.SemaphoreType.DMA(())],
    )
    return pl.pallas_call(
        _moe_ffn_kernel,
        grid_spec=grid_spec,
        out_shape=jax.ShapeDtypeStruct((nb * MOE_ROWS, d), F32),
        compiler_params=_params("arbitrary", "arbitrary"),
    )(block_e, slot_tok.reshape(nb, 1, MOE_ROWS), slot_gate.reshape(nb * MOE_ROWS, 1), n, e_gate, e_up, e_down)


def _combine_kernel(d_ref, h_ref, g_ref, y_hbm, o_ref, buf_ref, sem, *, tm, final):
    _gather_rows(d_ref, y_hbm, buf_ref, sem, TOP_K * tm)
    hh = h_ref[...]
    for k in range(TOP_K):
        hh = hh + buf_ref[pl.ds(k * tm, tm), :]
    o_ref[...] = _rms(hh, g_ref[...]) if final else hh


def moe_combine(h, y_slots, dest, g, final):
    m, d = h.shape
    tm = _pick(m, (384, 128, 8))
    nt = m // tm
    idx = dest.reshape(nt, tm, TOP_K).transpose(0, 2, 1).reshape(nt, 1, TOP_K * tm)
    return pl.pallas_call(
        functools.partial(_combine_kernel, tm=tm, final=final),
        grid=(nt,),
        in_specs=[
            pl.BlockSpec((1, 1, TOP_K * tm), lambda i: (i, 0, 0), memory_space=pltpu.SMEM),
            pl.BlockSpec((tm, d), lambda i: (i, 0)),
            pl.BlockSpec((1, d), lambda i: (0, 0)),
            pl.BlockSpec(memory_space=pl.ANY),
        ],
        out_specs=pl.BlockSpec((tm, d), lambda i: (i, 0)),
        out_shape=jax.ShapeDtypeStruct((m, d), F32),
        scratch_shapes=[pltpu.VMEM((TOP_K * tm, d), F32), pltpu.SemaphoreType.DMA(())],
        compiler_params=_params("arbitrary"),
    )(idx, h, g.reshape(1, d), y_slots)


def _route(logits):
    t = logits.shape[0]
    top_l, top_e = lax.top_k(logits, TOP_K)
    gates = jax.nn.softmax(top_l, -1).reshape(-1)
    flat_e = top_e.reshape(-1).astype(jnp.int32)
    onehot = (flat_e[:, None] == jnp.arange(N_EXPERTS, dtype=jnp.int32)[None, :]).astype(jnp.int32)
    csum = jnp.cumsum(onehot, axis=0)
    rank = jnp.sum(csum * onehot, axis=1) - 1
    counts = csum[-1]
    padded = (counts + MOE_ROWS - 1) // MOE_ROWS * MOE_ROWS
    pad_end = jnp.cumsum(padded)
    dest = (pad_end - padded)[flat_e] + rank
    nb = (t * TOP_K + N_EXPERTS * (MOE_ROWS - 1) + MOE_ROWS - 1) // MOE_ROWS
    slot_tok = jnp.zeros((nb * MOE_ROWS,), jnp.int32).at[dest].set(jnp.arange(t * TOP_K, dtype=jnp.int32) // TOP_K)
    slot_gate = jnp.zeros((nb * MOE_ROWS,), F32).at[dest].set(gates)
    block_e = jnp.minimum(jnp.searchsorted(pad_end, jnp.arange(nb, dtype=jnp.int32) * MOE_ROWS, side='right'),
                          N_EXPERTS - 1).astype(jnp.int32)
    return block_e, slot_tok, slot_gate, dest.reshape(t, TOP_K).astype(jnp.int32)


def _to_key_lanes(x, bn, L):
    xt = x.reshape(bn, L, A_HEADS, A_HEAD_DIM).transpose(1, 3, 0, 2).reshape(L, A_HEAD_DIM, bn * A_HEADS)
    return jnp.concatenate([xt, xt], -1).reshape(L, 8, 8, 2 * bn * A_HEADS)


def rwkv7_mixer(z, bn, L, mu, w0, w2, a0, a2, g2, k_k, k_a, r_k, lnx_w, lnx_b):
    z = z.reshape(bn, L, A_IN)
    z = z + (jnp.pad(z, ((0, 0), (1, 0), (0, 0)))[:, :-1] - z) * mu
    z = z.reshape(bn * L, A_IN)
    r, k, v = z[:, :A_WIDTH], z[:, A_WIDTH:2 * A_WIDTH], z[:, 2 * A_WIDTH:3 * A_WIDTH]
    o = 3 * A_WIDTH
    wd, ad, gd = z[:, o:o + W_LORA], z[:, o + W_LORA:o + W_LORA + A_LORA], z[:, o + W_LORA + A_LORA:]
    w_log = -jax.nn.softplus(-(w0 + act_matmul(wd, w2, "tanh"))) - 0.5
    dec = jnp.exp(-jnp.exp(w_log))
    a = jax.nn.sigmoid(a0 + act_matmul(ad, a2, "none"))
    g = act_matmul(gd, g2, "sigmoid")
    hs = lambda t: t.reshape(bn * L, A_HEADS, A_HEAD_DIM)
    kk = hs(k * k_k)
    kk = (kk / jnp.maximum(jnp.sqrt(jnp.sum(kk * kk, -1, keepdims=True)), 1e-12)).reshape(bn * L, A_WIDTH)
    k = k * (1.0 + (a - 1.0) * k_a)
    half = A_HEAD_DIM // 2
    v_s = v.reshape(bn, L, A_HEADS, 2, half).transpose(1, 4, 3, 0, 2).reshape(L, half, 2 * bn * A_HEADS)
    y_s = wkv_scan(_to_key_lanes(-kk, bn, L), _to_key_lanes(dec, bn, L), _to_key_lanes(kk * a, bn, L),
                   _to_key_lanes(k, bn, L), _to_key_lanes(r, bn, L), v_s)
    y = y_s.reshape(L, half, 2, bn, A_HEADS).transpose(3, 0, 4, 2, 1).reshape(bn * L, A_HEADS, A_HEAD_DIM)
    mean = jnp.mean(y, -1, keepdims=True)
    var = jnp.mean(jnp.square(y - mean), -1, keepdims=True)
    y = ((y - mean) * lax.rsqrt(var + LNX_EPS)).reshape(bn * L, A_WIDTH) * lnx_w + lnx_b
    bonus = jnp.sum(hs(r) * hs(k) * r_k, -1, keepdims=True) * hs(v)
    return (y + bonus.reshape(bn * L, A_WIDTH)) * g


def kernel(x, meta_tokens, mix_norm_g, w_in, gate_b, a_mu, a_w0, a_w2, a_a0, a_a2, a_g2, a_kk, a_ka, a_rk, a_lnx_w, a_lnx_b, b_conv_w, b_conv_b, b_ln_g, b_ln_b, c_pool_w, c_scale, proj_a, proj_b, proj_c, w_out, ffn_norm_g, ffn_gate, ffn_up, ffn_down, router_w, router_b, exp_gate, exp_up, exp_down, final_norm_g):
    bn, seq, d = x.shape
    depth = w_in.shape[0]
    L = N_META + seq
    assert 2 * bn * A_HEADS == V7X_LANES, "the recurrence kernel maps (value half, batch, head) onto the 128 lanes"
    bf = lambda w: w.astype(BF16)
    meta = jnp.broadcast_to(meta_tokens[None].astype(x.dtype), (bn, N_META, d))
    h = jnp.concatenate([meta, x], 1).reshape(bn * L, d)
    s_a, s_b, s_c = A_IN, A_IN + 2 * B_WIDTH, A_IN + 2 * B_WIDTH + C_WIDTH
    out = None
    for i in range(depth):
        w_i = bf(w_in[i])
        z_a = norm_matmul(h, mix_norm_g[i], w_i[:, :s_a])
        z_b = norm_matmul(h, mix_norm_g[i], w_i[:, s_a:s_b])
        z_c = norm_matmul(h, mix_norm_g[i], w_i[:, s_b:s_c])
        z_g = norm_matmul(h, mix_norm_g[i], w_i[:, s_c:])
        y_a = rwkv7_mixer(z_a, bn, L, a_mu[i], a_w0[i], bf(a_w2[i]), a_a0[i], bf(a_a2[i]), bf(a_g2[i]),
                          a_kk[i], a_ka[i], a_rk[i], a_lnx_w[i], a_lnx_b[i])
        y_b = conformer_conv(z_b.reshape(bn, L, 2 * B_WIDTH), b_conv_w[i], b_conv_b[i], b_ln_g[i], b_ln_b[i])
        y_c = multiscale_pool(z_c.reshape(bn, L, C_WIDTH), bf(c_pool_w[i]), c_scale[i])
        h = merge(y_a, y_b.reshape(bn * L, B_WIDTH), y_c.reshape(bn * L, C_WIDTH), z_g, gate_b[i], h,
                  bf(proj_a[i]), bf(proj_b[i]), bf(proj_c[i]), bf(w_out[i]))
        last = i == depth - 1
        j = i // 2
        if i % 2 == 0:
            h = dense_ffn(h, ffn_norm_g[i], bf(ffn_gate[j]), bf(ffn_up[j]), bf(ffn_down[j]), final_norm_g, last)
        else:
            n, logits = router(h, ffn_norm_g[i], router_w[j], router_b[j])
            block_e, slot_tok, slot_gate, dest = _route(logits[:, :N_EXPERTS])
            y_slots = moe_expert_ffn(n, block_e, slot_tok, slot_gate, bf(exp_gate[j]), bf(exp_up[j]), bf(exp_down[j]))
            h = moe_combine(h, y_slots, dest, final_norm_g, last)
    return h.reshape(bn, L, d)[:, N_META:]
```

```python
import functools

import jax
import jax.numpy as jnp
from jax import lax
from jax.experimental import pallas as pl
from jax.experimental.pallas import tpu as pltpu

F32 = jnp.float32
BF16 = jnp.bfloat16

N_META = 16
A_HEADS = 8
A_HEAD_DIM = 64
A_WIDTH = A_HEADS * A_HEAD_DIM
W_LORA = 64
A_LORA = 64
G_LORA = 128
A_IN = 3 * A_WIDTH + W_LORA + A_LORA + G_LORA
LNX_EPS = 64e-5
B_WIDTH = 512
CONV_WIDTH = 31
C_WIDTH = 512
POOL_WINDOWS = (2, 4, 8, 16)
C_GROUP_DIM = C_WIDTH // len(POOL_WINDOWS)
N_EXPERTS = 8
TOP_K = 2
RMS_EPS = 1e-6
LN_EPS = 1e-5

V7X_LANES = 128
V7X_SUBLANES = 8
VMEM_LIMIT_BYTES = 58 * 1024 * 1024
ROUTE_LANES = V7X_LANES
MOE_ROWS = 512
CONV_HIST = 32
POOL_HIST = 16
CONV_ROWS = 16

R0, K0, V0 = 0, A_WIDTH, 2 * A_WIDTH
WD0 = 3 * A_WIDTH
AD0 = WD0 + V7X_LANES
GD0 = AD0 + V7X_LANES
A_END = GD0 + G_LORA
BL0 = A_END
BG0 = BL0 + B_WIDTH
C0 = BG0 + B_WIDTH
P_END = C0 + C_WIDTH


def _pick(n, candidates):
    for c in candidates:
        if n % c == 0:
            return c
    raise ValueError(f"no tile in {candidates} divides {n}")


def _params(*sem):
    return pltpu.CompilerParams(dimension_semantics=sem, vmem_limit_bytes=VMEM_LIMIT_BYTES)


def _rms(x, g):
    return x * lax.rsqrt(jnp.mean(x * x, -1, keepdims=True) + RMS_EPS) * g


def _silu(x):
    return x * jax.nn.sigmoid(x)


def _softplus(x):
    return jnp.maximum(x, 0.0) + jnp.log(1.0 + jnp.exp(-jnp.abs(x)))


def _bdot(x, w):
    return jnp.dot(x.astype(BF16), w, preferred_element_type=F32)


def _head_sum(x, ones_ref):
    hi = x.astype(BF16)
    lo = (x - hi.astype(F32)).astype(BF16)
    ones = ones_ref[...]
    return jnp.dot(hi, ones, preferred_element_type=F32) + jnp.dot(lo, ones, preferred_element_type=F32)


def _prologue_kernel(h_ref, ng_ref, w_ref, mu_ref, w0_ref, w2_ref, a0_ref, a2_ref, g2_ref, kk_ref, ka_ref, rk_ref,
                     cw_ref, cb_ref, lg_ref, lb_ref, pw_ref, ps_ref, ones_ref,
                     na_ref, dec_ref, bb_ref, k2_ref, r_ref, v_ref, bonus_ref, g_ref, yb_ref, yc_ref,
                     zbuf_ref, cbuf_ref, sbuf_ref, pbuf_ref, *, tt):
    i = pl.program_id(1)

    @pl.when(i == 0)
    def _():
        zbuf_ref[pl.ds(0, V7X_SUBLANES), :] = jnp.zeros((V7X_SUBLANES, A_END), F32)
        cbuf_ref[pl.ds(0, CONV_HIST), :] = jnp.zeros((CONV_HIST, B_WIDTH), F32)
        pbuf_ref[pl.ds(0, POOL_HIST), :] = jnp.zeros((POOL_HIST, C_WIDTH), F32)

    n = _rms(h_ref[0], ng_ref[...]).astype(BF16)

    za = jnp.dot(n, w_ref[:, 0:A_END], preferred_element_type=F32)
    zbuf_ref[pl.ds(V7X_SUBLANES, tt), :] = za
    prev = zbuf_ref[pl.ds(V7X_SUBLANES - 1, tt), :]
    zbuf_ref[pl.ds(V7X_SUBLANES - 1, 1), :] = zbuf_ref[pl.ds(V7X_SUBLANES - 1 + tt, 1), :]
    za = za + (prev - za) * mu_ref[...]
    r = za[:, R0:R0 + A_WIDTH]
    k = za[:, K0:K0 + A_WIDTH]
    v = za[:, V0:V0 + A_WIDTH]
    w_log = -_softplus(-(w0_ref[...] + _bdot(jnp.tanh(za[:, WD0:AD0]), w2_ref[...]))) - 0.5
    dec_ref[0] = jnp.exp(-jnp.exp(w_log))
    a = jax.nn.sigmoid(a0_ref[...] + _bdot(za[:, AD0:GD0], a2_ref[...]))
    g_ref[0] = _bdot(jax.nn.sigmoid(za[:, GD0:A_END]), g2_ref[...])
    kk = k * kk_ref[...]
    kk = kk / jnp.maximum(jnp.sqrt(_head_sum(kk * kk, ones_ref)), 1e-12)
    k2 = k * (1.0 + (a - 1.0) * ka_ref[...])
    na_ref[0] = -kk
    bb_ref[0] = kk * a
    k2_ref[0] = k2
    r_ref[0] = r
    v_ref[0] = v
    bonus_ref[0] = _head_sum(r * k2 * rk_ref[...], ones_ref) * v

    glu = (jnp.dot(n, w_ref[:, BL0:BG0], preferred_element_type=F32)
           * jax.nn.sigmoid(jnp.dot(n, w_ref[:, BG0:C0], preferred_element_type=F32)))
    cbuf_ref[pl.ds(CONV_HIST, tt), :] = glu
    span = tt + CONV_HIST - V7X_SUBLANES
    for s in range(1, V7X_SUBLANES):
        sbuf_ref[s - 1] = cbuf_ref[pl.ds(s, span), :]
    first = CONV_HIST - (CONV_WIDTH - 1)

    def conv_rows(c, carry):
        base = pl.multiple_of(c * CONV_ROWS, CONV_ROWS)
        acc = jnp.zeros((CONV_ROWS, B_WIDTH), F32) + cb_ref[...]
        for j in range(CONV_WIDTH):
            q, s = divmod(first + j, V7X_SUBLANES)
            rows = pl.ds(base + q * V7X_SUBLANES, CONV_ROWS)
            tap = cbuf_ref[rows, :] if s == 0 else sbuf_ref[s - 1, rows, :]
            acc = acc + cw_ref[pl.ds(j, 1), :] * tap
        cen = acc - jnp.mean(acc, -1, keepdims=True)
        var = jnp.mean(cen * cen, -1, keepdims=True)
        yb_ref[0, pl.ds(base, CONV_ROWS), :] = _silu(cen * lax.rsqrt(var + LN_EPS) * lg_ref[...] + lb_ref[...])
        return carry

    lax.fori_loop(0, tt // CONV_ROWS, conv_rows, 0)
    cbuf_ref[pl.ds(0, CONV_HIST), :] = cbuf_ref[pl.ds(tt, CONV_HIST), :]

    pbuf_ref[pl.ds(POOL_HIST, tt), :] = jnp.dot(n, w_ref[:, C0:P_END], preferred_element_type=F32)
    pos = (lax.broadcasted_iota(jnp.int32, (tt, C_GROUP_DIM), 0) + (i * tt + 1)).astype(F32)
    for gi, win in enumerate(POOL_WINDOWS):
        lanes = pl.ds(gi * C_GROUP_DIM, C_GROUP_DIM)
        cur = pbuf_ref[pl.ds(POOL_HIST, tt), lanes]
        tot = cur
        for d in range(1, win):
            tot = tot + pbuf_ref[pl.ds(POOL_HIST - d, tt), lanes]
        pooled = tot / jnp.minimum(pos, float(win)) - cur
        yc_ref[0, :, lanes] = _bdot(pooled, pw_ref[gi]) * ps_ref[:, lanes]
    pbuf_ref[pl.ds(0, POOL_HIST), :] = pbuf_ref[pl.ds(tt, POOL_HIST), :]


def mixer_prologue(h, bn, L, p):
    d = h.shape[-1]
    tt = _pick(L, (304, 144, 48))
    assert tt % CONV_ROWS == 0
    const = lambda a: pl.BlockSpec(a.shape, lambda b, i: (0,) * a.ndim)
    consts = [p["norm_g"], p["w_p"], p["mu"], p["w0"], p["w2"], p["a0"], p["a2"], p["g2"], p["k_k"], p["k_a"],
              p["r_k"], p["conv_w"], p["conv_b"], p["ln_g"], p["ln_b"], p["pool_w"], p["pool_s"], p["ones"]]
    out_spec = pl.BlockSpec((1, tt, A_WIDTH), lambda b, i: (b, i, 0))
    n_out = 10
    return pl.pallas_call(
        functools.partial(_prologue_kernel, tt=tt),
        grid=(bn, L // tt),
        in_specs=[pl.BlockSpec((1, tt, d), lambda b, i: (b, i, 0))] + [const(a) for a in consts],
        out_specs=[out_spec] * n_out,
        out_shape=[jax.ShapeDtypeStruct((bn, L, A_WIDTH), F32)] * n_out,
        scratch_shapes=[
            pltpu.VMEM((tt + V7X_SUBLANES, A_END), F32),
            pltpu.VMEM((tt + CONV_HIST, B_WIDTH), F32),
            pltpu.VMEM((V7X_SUBLANES - 1, tt + CONV_HIST - V7X_SUBLANES, B_WIDTH), F32),
            pltpu.VMEM((tt + POOL_HIST, C_WIDTH), F32),
        ],
        compiler_params=_params("parallel", "arbitrary"),
        name="mixer_prologue",
    )(h.reshape(bn, L, d), *consts)


def _sublane_total(x):
    p = jnp.sum(x, axis=0)
    p = p + pltpu.roll(p, 4, 0)
    p = p + pltpu.roll(p, 2, 0)
    return p + pltpu.roll(p, 1, 0)


def _wkv_kernel(a_ref, w_ref, b_ref, k_ref, r_ref, v_ref, y_ref, s_ref, *, tt, nv):
    @pl.when(pl.program_id(0) == 0)
    def _():
        s_ref[...] = jnp.zeros_like(s_ref)

    def step(t, carry):
        a = a_ref[t]
        w = w_ref[t]
        b = b_ref[t]
        k = k_ref[t]
        r = r_ref[t]
        wr = w * r
        br = _sublane_total(b * r)
        kr = _sublane_total(k * r)
        for vi in range(nv):
            s = s_ref[vi]
            vrow = v_ref[t, pl.ds(vi, 1), :]
            sa = _sublane_total(s * a)
            y0 = _sublane_total(s * wr)
            s_ref[vi] = s * w + sa[None] * b + vrow[None] * k
            y = y0 + sa * br + vrow * kr
            y_ref[t, pl.ds(vi, 1), :] = y[0:1]
        return carry

    lax.fori_loop(0, tt, step, 0)


def wkv_scan(a_s, w_s, b_s, k_s, r_s, v_s):
    L = a_s.shape[0]
    nv = v_s.shape[1]
    tt = _pick(L, (48, 16, 8, 1))
    vec = pl.BlockSpec((tt, 8, 8, V7X_LANES), lambda i: (i, 0, 0, 0))
    row = pl.BlockSpec((tt, nv, V7X_LANES), lambda i: (i, 0, 0))
    return pl.pallas_call(
        functools.partial(_wkv_kernel, tt=tt, nv=nv),
        grid=(L // tt,),
        in_specs=[vec, vec, vec, vec, vec, row],
        out_specs=row,
        out_shape=jax.ShapeDtypeStruct((L, nv, V7X_LANES), F32),
        scratch_shapes=[pltpu.VMEM((nv, 8, 8, V7X_LANES), F32)],
        compiler_params=_params("arbitrary"),
        name="wkv_scan",
    )(a_s, w_s, b_s, k_s, r_s, v_s)


def _to_key_lanes(x, bn, L):
    xt = x.reshape(bn, L, A_HEADS, A_HEAD_DIM).transpose(1, 3, 0, 2).reshape(L, A_HEAD_DIM, bn * A_HEADS)
    return jnp.concatenate([xt, xt], -1).reshape(L, 8, 8, 2 * bn * A_HEADS)


def wkv_natural(na, dec, bb, k2, r, v, bn, L):
    half = A_HEAD_DIM // 2
    v_s = v.reshape(bn, L, A_HEADS, 2, half).transpose(1, 4, 3, 0, 2).reshape(L, half, 2 * bn * A_HEADS)
    y_s = wkv_scan(_to_key_lanes(na, bn, L), _to_key_lanes(dec, bn, L), _to_key_lanes(bb, bn, L),
                   _to_key_lanes(k2, bn, L), _to_key_lanes(r, bn, L), v_s)
    return y_s.reshape(L, half, 2, bn, A_HEADS).transpose(3, 0, 4, 2, 1).reshape(bn * L, A_WIDTH)


def _merge_kernel(y_ref, bonus_ref, g_ref, yb_ref, yc_ref, h_ref, ng_ref, wg_ref, gb_ref, lw_ref, lb_ref,
                  pa_ref, pb_ref, pc_ref, wo_ref, ones_ref, *rest, route):
    if route:
        fg_ref, rw_ref, rb_ref, o_ref, n_ref, l_ref = rest
    else:
        (o_ref,) = rest
    d = h_ref.shape[-1]
    h = h_ref[...]
    gates = jax.nn.sigmoid(jnp.dot(_rms(h, ng_ref[...]).astype(BF16), wg_ref[...], preferred_element_type=F32)
                           + gb_ref[...])
    y = y_ref[...]
    cen = y - _head_sum(y, ones_ref) * (1.0 / A_HEAD_DIM)
    var = _head_sum(cen * cen, ones_ref) * (1.0 / A_HEAD_DIM)
    ya = (cen * lax.rsqrt(var + LNX_EPS) * lw_ref[...] + lb_ref[...] + bonus_ref[...]) * g_ref[...]
    merged = (gates[:, :d] * _bdot(ya, pa_ref[...]) + gates[:, d:2 * d] * _bdot(yb_ref[...], pb_ref[...])
              + gates[:, 2 * d:] * _bdot(yc_ref[...], pc_ref[...]))
    out = h + _bdot(merged, wo_ref[...])
    o_ref[...] = out
    if route:
        n = _rms(out, fg_ref[...])
        n_ref[...] = n
        l_ref[...] = jnp.dot(n, rw_ref[...], preferred_element_type=F32,
                             precision=lax.Precision.HIGHEST) + rb_ref[...]


def mixer_merge(y, bonus, g, y_b, y_c, h, p, route):
    m, d = h.shape
    tm = _pick(m, (384, 128, 8))
    rows = lambda width: pl.BlockSpec((tm, width), lambda i: (i, 0))
    const = lambda a: pl.BlockSpec(a.shape, lambda i: (0,) * a.ndim)
    consts = [p["norm_g"], p["w_g"], p["gate_b"], p["lnx_w"], p["lnx_b"], p["proj_a"], p["proj_b"], p["proj_c"],
              p["w_out"], p["ones"]]
    out_specs = [rows(d)]
    out_shape = [jax.ShapeDtypeStruct((m, d), F32)]
    if route:
        consts += [p["ffn_g"], p["router_w"], p["router_b"]]
        out_specs += [rows(d), rows(ROUTE_LANES)]
        out_shape += [jax.ShapeDtypeStruct((m, d), F32), jax.ShapeDtypeStruct((m, ROUTE_LANES), F32)]
    return pl.pallas_call(
        functools.partial(_merge_kernel, route=route),
        grid=(m // tm,),
        in_specs=[rows(A_WIDTH)] * 3 + [rows(B_WIDTH), rows(C_WIDTH), rows(d)] + [const(a) for a in consts],
        out_specs=out_specs,
        out_shape=out_shape,
        compiler_params=_params("parallel"),
        name="mixer_merge",
    )(y, bonus, g, y_b, y_c, h, *consts)


def _ffn_kernel(h_ref, g_ref, fg_ref, wg_ref, wu_ref, wd_ref, o_ref, xn_ref, acc_ref, *, final):
    j = pl.program_id(1)

    @pl.when(j == 0)
    def _():
        xn_ref[...] = _rms(h_ref[...], g_ref[...]).astype(BF16)
        acc_ref[...] = jnp.zeros_like(acc_ref)

    x = xn_ref[...]
    gate = jnp.dot(x, wg_ref[...], preferred_element_type=F32)
    up = jnp.dot(x, wu_ref[...], preferred_element_type=F32)
    acc_ref[...] += _bdot(_silu(gate) * up, wd_ref[...])

    @pl.when(j == pl.num_programs(1) - 1)
    def _():
        hh = h_ref[...] + acc_ref[...]
        o_ref[...] = _rms(hh, fg_ref[...]) if final else hh


def dense_ffn(h, g, w_gate, w_up, w_down, final_g, final):
    m, d = h.shape
    ff = w_gate.shape[1]
    tm = _pick(m, (576, 384, 128, 8))
    tf = _pick(ff, (1408, 256, 128))
    return pl.pallas_call(
        functools.partial(_ffn_kernel, final=final),
        grid=(m // tm, ff // tf),
        in_specs=[
            pl.BlockSpec((tm, d), lambda i, j: (i, 0)),
            pl.BlockSpec((1, d), lambda i, j: (0, 0)),
            pl.BlockSpec((1, d), lambda i, j: (0, 0)),
            pl.BlockSpec((d, tf), lambda i, j: (0, j)),
            pl.BlockSpec((d, tf), lambda i, j: (0, j)),
            pl.BlockSpec((tf, d), lambda i, j: (j, 0)),
        ],
        out_specs=pl.BlockSpec((tm, d), lambda i, j: (i, 0)),
        out_shape=jax.ShapeDtypeStruct((m, d), F32),
        scratch_shapes=[pltpu.VMEM((tm, d), BF16), pltpu.VMEM((tm, d), F32)],
        compiler_params=_params("parallel", "arbitrary"),
        name="dense_ffn",
    )(h, g.reshape(1, d), final_g.reshape(1, d), w_gate, w_up, w_down)


def _row_copy(idx_ref, src_hbm, dst_ref, sem, r):
    return pltpu.make_async_copy(src_hbm.at[pl.ds(idx_ref[0, 0, r], 1)], dst_ref.at[pl.ds(r, 1)], sem)


def _start_row_gather(idx_ref, src_hbm, dst_ref, sem, n_rows):
    def start(r, c):
        _row_copy(idx_ref, src_hbm, dst_ref, sem, r).start()
        return c

    lax.fori_loop(0, n_rows, start, 0, unroll=8)


def _wait_row_gather(idx_ref, src_hbm, dst_ref, sem, n_rows):
    def wait(r, c):
        _row_copy(idx_ref, src_hbm, dst_ref, sem, r).wait()
        return c

    lax.fori_loop(0, n_rows, wait, 0, unroll=8)


def _moe_ffn_kernel(be_ref, tok_ref, nxt_ref, sg_ref, x_hbm, wg_ref, wu_ref, wd_ref, o_ref,
                    xraw_ref, x_ref, acc_ref, sem):
    del be_ref
    i = pl.program_id(0)
    j = pl.program_id(1)
    slot = i % 2

    @pl.when(j == 0)
    def _():
        @pl.when(i == 0)
        def _():
            _start_row_gather(tok_ref, x_hbm, xraw_ref.at[0], sem.at[0], MOE_ROWS)

        _wait_row_gather(tok_ref, x_hbm, xraw_ref.at[slot], sem.at[slot], MOE_ROWS)

        @pl.when(i + 1 < pl.num_programs(0))
        def _():
            _start_row_gather(nxt_ref, x_hbm, xraw_ref.at[1 - slot], sem.at[1 - slot], MOE_ROWS)

        x_ref[...] = xraw_ref[slot].astype(BF16)
        acc_ref[...] = jnp.zeros_like(acc_ref)

    x = x_ref[...]
    gate = jnp.dot(x, wg_ref[0], preferred_element_type=F32)
    up = jnp.dot(x, wu_ref[0], preferred_element_type=F32)
    acc_ref[...] += _bdot(_silu(gate) * up, wd_ref[0])

    @pl.when(j == pl.num_programs(1) - 1)
    def _():
        o_ref[...] = acc_ref[...] * sg_ref[...]


def moe_expert_ffn(n, block_e, slot_tok, slot_gate, e_gate, e_up, e_down):
    d = n.shape[1]
    ff = e_gate.shape[2]
    nb = block_e.shape[0]
    tf = _pick(ff, (1408, 256, 128))
    tok = slot_tok.reshape(nb, 1, MOE_ROWS)
    grid_spec = pltpu.PrefetchScalarGridSpec(
        num_scalar_prefetch=1,
        grid=(nb, ff // tf),
        in_specs=[
            pl.BlockSpec((1, 1, MOE_ROWS), lambda i, j, be: (i, 0, 0), memory_space=pltpu.SMEM),
            pl.BlockSpec((1, 1, MOE_ROWS), lambda i, j, be: (jnp.minimum(i + 1, nb - 1), 0, 0),
                         memory_space=pltpu.SMEM),
            pl.BlockSpec((MOE_ROWS, 1), lambda i, j, be: (i, 0)),
            pl.BlockSpec(memory_space=pl.ANY),
            pl.BlockSpec((1, d, tf), lambda i, j, be: (be[i], 0, j)),
            pl.BlockSpec((1, d, tf), lambda i, j, be: (be[i], 0, j)),
            pl.BlockSpec((1, tf, d), lambda i, j, be: (be[i], j, 0)),
        ],
        out_specs=pl.BlockSpec((MOE_ROWS, d), lambda i, j, be: (i, 0)),
        scratch_shapes=[pltpu.VMEM((2, MOE_ROWS, d), F32), pltpu.VMEM((MOE_ROWS, d), BF16),
                        pltpu.VMEM((MOE_ROWS, d), F32), pltpu.SemaphoreType.DMA((2,))],
    )
    return pl.pallas_call(
        _moe_ffn_kernel,
        grid_spec=grid_spec,
        out_shape=jax.ShapeDtypeStruct((nb * MOE_ROWS, d), F32),
        compiler_params=_params("arbitrary", "arbitrary"),
        name="moe_expert_ffn",
    )(block_e, tok, tok, slot_gate.reshape(nb * MOE_ROWS, 1), n, e_gate, e_up, e_down)


def _combine_kernel(d_ref, h_ref, g_ref, y_hbm, o_ref, buf_ref, sem, *, tm, final):
    _start_row_gather(d_ref, y_hbm, buf_ref, sem, TOP_K * tm)
    _wait_row_gather(d_ref, y_hbm, buf_ref, sem, TOP_K * tm)
    hh = h_ref[...]
    for k in range(TOP_K):
        hh = hh + buf_ref[pl.ds(k * tm, tm), :]
    o_ref[...] = _rms(hh, g_ref[...]) if final else hh


def moe_combine(h, y_slots, dest, g, final):
    m, d = h.shape
    tm = _pick(m, (384, 128, 8))
    nt = m // tm
    idx = dest.reshape(nt, tm, TOP_K).transpose(0, 2, 1).reshape(nt, 1, TOP_K * tm)
    return pl.pallas_call(
        functools.partial(_combine_kernel, tm=tm, final=final),
        grid=(nt,),
        in_specs=[
            pl.BlockSpec((1, 1, TOP_K * tm), lambda i: (i, 0, 0), memory_space=pltpu.SMEM),
            pl.BlockSpec((tm, d), lambda i: (i, 0)),
            pl.BlockSpec((1, d), lambda i: (0, 0)),
            pl.BlockSpec(memory_space=pl.ANY),
        ],
        out_specs=pl.BlockSpec((tm, d), lambda i: (i, 0)),
        out_shape=jax.ShapeDtypeStruct((m, d), F32),
        scratch_shapes=[pltpu.VMEM((TOP_K * tm, d), F32), pltpu.SemaphoreType.DMA(())],
        compiler_params=_params("arbitrary"),
        name="moe_combine",
    )(idx, h, g.reshape(1, d), y_slots)


def _route(logits):
    t = logits.shape[0]
    top_l, top_e = lax.top_k(logits, TOP_K)
    gates = jax.nn.softmax(top_l, -1).reshape(-1)
    flat_e = top_e.reshape(-1).astype(jnp.int32)
    onehot = (flat_e[:, None] == jnp.arange(N_EXPERTS, dtype=jnp.int32)[None, :]).astype(jnp.int32)
    csum = jnp.cumsum(onehot, axis=0)
    rank = jnp.sum(csum * onehot, axis=1) - 1
    counts = csum[-1]
    padded = (counts + MOE_ROWS - 1) // MOE_ROWS * MOE_ROWS
    pad_end = jnp.cumsum(padded)
    dest = (pad_end - padded)[flat_e] + rank
    nb = (t * TOP_K + N_EXPERTS * (MOE_ROWS - 1) + MOE_ROWS - 1) // MOE_ROWS
    slot_tok = jnp.zeros((nb * MOE_ROWS,), jnp.int32).at[dest].set(jnp.arange(t * TOP_K, dtype=jnp.int32) // TOP_K)
    slot_gate = jnp.zeros((nb * MOE_ROWS,), F32).at[dest].set(gates)
    block_e = jnp.minimum(jnp.searchsorted(pad_end, jnp.arange(nb, dtype=jnp.int32) * MOE_ROWS, side='right'),
                          N_EXPERTS - 1).astype(jnp.int32)
    return block_e, slot_tok, slot_gate, dest.reshape(t, TOP_K).astype(jnp.int32)


def _layer_params(i, d, mix_norm_g, w_in, gate_b, a_mu, a_w0, a_w2, a_a0, a_a2, a_g2, a_kk, a_ka, a_rk, a_lnx_w,
                  a_lnx_b, b_conv_w, b_conv_b, b_ln_g, b_ln_b, c_pool_w, c_scale, proj_a, proj_b, proj_c, w_out):
    bf = lambda w: w.astype(BF16)
    row = lambda a: a.reshape(1, -1)
    s_a, s_b, s_c = A_IN, A_IN + 2 * B_WIDTH, A_IN + 2 * B_WIDTH + C_WIDTH

    def lane_pad(a, axis):
        o = 3 * A_WIDTH
        take = lambda lo, hi: lax.slice_in_dim(a, lo, hi, axis=axis)
        zshape = list(a.shape)
        zshape[axis] = V7X_LANES - W_LORA
        z = jnp.zeros(zshape, a.dtype)
        return jnp.concatenate([take(0, o), take(o, o + W_LORA), z, take(o + W_LORA, o + W_LORA + A_LORA), z,
                                take(o + W_LORA + A_LORA, s_a)], axis)

    w = w_in[i]
    w_p = jnp.concatenate([lane_pad(w[:, :s_a], 1), w[:, s_a:s_c]], 1)
    pad_rows = lambda a: jnp.concatenate([a, jnp.zeros((V7X_LANES - a.shape[0], a.shape[1]), a.dtype)], 0)
    head = jnp.arange(A_WIDTH, dtype=jnp.int32) // A_HEAD_DIM
    return {
        "norm_g": row(mix_norm_g[i]), "w_p": bf(w_p), "mu": row(lane_pad(a_mu[i], 0)),
        "w0": row(a_w0[i]), "w2": bf(pad_rows(a_w2[i])), "a0": row(a_a0[i]), "a2": bf(pad_rows(a_a2[i])),
        "g2": bf(a_g2[i]), "k_k": row(a_kk[i]), "k_a": row(a_ka[i]), "r_k": row(a_rk[i]),
        "conv_w": b_conv_w[i].reshape(CONV_WIDTH, B_WIDTH), "conv_b": row(b_conv_b[i]),
        "ln_g": row(b_ln_g[i]), "ln_b": row(b_ln_b[i]), "pool_w": bf(c_pool_w[i]), "pool_s": row(c_scale[i]),
        "ones": (head[:, None] == head[None, :]).astype(BF16),
        "w_g": bf(w[:, s_c:]), "gate_b": row(gate_b[i]), "lnx_w": row(a_lnx_w[i]), "lnx_b": row(a_lnx_b[i]),
        "proj_a": bf(proj_a[i]), "proj_b": bf(proj_b[i]), "proj_c": bf(proj_c[i]), "w_out": bf(w_out[i]),
    }


def kernel(x, meta_tokens, mix_norm_g, w_in, gate_b, a_mu, a_w0, a_w2, a_a0, a_a2, a_g2, a_kk, a_ka, a_rk, a_lnx_w, a_lnx_b, b_conv_w, b_conv_b, b_ln_g, b_ln_b, c_pool_w, c_scale, proj_a, proj_b, proj_c, w_out, ffn_norm_g, ffn_gate, ffn_up, ffn_down, router_w, router_b, exp_gate, exp_up, exp_down, final_norm_g):
    bn, seq, d = x.shape
    depth = w_in.shape[0]
    L = N_META + seq
    assert 2 * bn * A_HEADS == V7X_LANES, "the recurrence kernel maps (value half, batch, head) onto the 128 lanes"
    bf = lambda w: w.astype(BF16)
    meta = jnp.broadcast_to(meta_tokens[None].astype(x.dtype), (bn, N_META, d))
    h = jnp.concatenate([meta, x], 1).reshape(bn * L, d)
    for i in range(depth):
        p = _layer_params(i, d, mix_norm_g, w_in, gate_b, a_mu, a_w0, a_w2, a_a0, a_a2, a_g2, a_kk, a_ka, a_rk,
                          a_lnx_w, a_lnx_b, b_conv_w, b_conv_b, b_ln_g, b_ln_b, c_pool_w, c_scale,
                          proj_a, proj_b, proj_c, w_out)
        last = i == depth - 1
        j = i // 2
        moe = i % 2 == 1
        if moe:
            p["ffn_g"] = ffn_norm_g[i].reshape(1, d)
            p["router_w"] = jnp.zeros((d, ROUTE_LANES), F32).at[:, :N_EXPERTS].set(router_w[j])
            p["router_b"] = jnp.zeros((1, ROUTE_LANES), F32).at[0, :N_EXPERTS].set(router_b[j])
        na, dec, bb, k2, r, v, bonus, g, y_b, y_c = mixer_prologue(h, bn, L, p)
        y = wkv_natural(na, dec, bb, k2, r, v, bn, L)
        flat = lambda a: a.reshape(bn * L, a.shape[-1])
        merged = mixer_merge(y, flat(bonus), flat(g), flat(y_b), flat(y_c), h, p, moe)
        if moe:
            h, n, logits = merged
            block_e, slot_tok, slot_gate, dest = _route(logits[:, :N_EXPERTS])
            y_slots = moe_expert_ffn(n, block_e, slot_tok, slot_gate, bf(exp_gate[j]), bf(exp_up[j]), bf(exp_down[j]))
            h = moe_combine(h, y_slots, dest, final_norm_g, last)
        else:
            (h,) = merged
            h = dense_ffn(h, ffn_norm_g[i], bf(ffn_gate[j]), bf(ffn_up[j]), bf(ffn_down[j]), final_norm_g, last)
    return h.reshape(bn, L, d)[:, N_META:]
```

```python
import functools

import jax
import jax.numpy as jnp
from jax import lax
from jax.experimental import pallas as pl
from jax.experimental.pallas import tpu as pltpu

F32 = jnp.float32
BF16 = jnp.bfloat16

N_META = 16
A_HEADS = 8
A_HEAD_DIM = 64
A_WIDTH = A_HEADS * A_HEAD_DIM
W_LORA = 64
A_LORA = 64
G_LORA = 128
A_IN = 3 * A_WIDTH + W_LORA + A_LORA + G_LORA
LNX_EPS = 64e-5
B_WIDTH = 512
CONV_WIDTH = 31
C_WIDTH = 512
POOL_WINDOWS = (2, 4, 8, 16)
C_GROUP_DIM = C_WIDTH // len(POOL_WINDOWS)
N_EXPERTS = 8
TOP_K = 2
RMS_EPS = 1e-6
LN_EPS = 1e-5

V7X_LANES = 128
V7X_SUBLANES = 8
VMEM_LIMIT_BYTES = 58 * 1024 * 1024
ROUTE_LANES = V7X_LANES
MOE_ROWS = 512
CONV_HIST = 32
POOL_HIST = 16
CONV_ROWS = 16
CONV_PARTS = 4

R0, K0, V0 = 0, A_WIDTH, 2 * A_WIDTH
WD0 = 3 * A_WIDTH
AD0 = WD0 + V7X_LANES
GD0 = AD0 + V7X_LANES
A_END = GD0 + G_LORA
BL0 = A_END
BG0 = BL0 + B_WIDTH
C0 = BG0 + B_WIDTH
P_END = C0 + C_WIDTH


def _pick(n, candidates):
    for c in candidates:
        if n % c == 0:
            return c
    raise ValueError(f"no tile in {candidates} divides {n}")


def _params(*sem):
    return pltpu.CompilerParams(dimension_semantics=sem, vmem_limit_bytes=VMEM_LIMIT_BYTES)


def _rms(x, g):
    return x * lax.rsqrt(jnp.mean(x * x, -1, keepdims=True) + RMS_EPS) * g


def _silu(x):
    return x * jax.nn.sigmoid(x)


def _softplus(x):
    return jnp.maximum(x, 0.0) + jnp.log(1.0 + jnp.exp(-jnp.abs(x)))


def _bdot(x, w):
    return jnp.dot(x.astype(BF16), w, preferred_element_type=F32)


def _head_sum(x, ones_ref):
    hi = x.astype(BF16)
    lo = (x - hi.astype(F32)).astype(BF16)
    ones = ones_ref[...]
    return jnp.dot(hi, ones, preferred_element_type=F32) + jnp.dot(lo, ones, preferred_element_type=F32)


def _prologue_kernel(h_ref, ng_ref, w_ref, mu_ref, w0_ref, w2_ref, a0_ref, a2_ref, g2_ref, kk_ref, ka_ref, rk_ref,
                     cw_ref, cb_ref, lg_ref, lb_ref, pw_ref, ps_ref, ones_ref,
                     na_ref, dec_ref, bb_ref, k2_ref, r_ref, v_ref, bonus_ref, g_ref, yb_ref, yc_ref,
                     zbuf_ref, cbuf_ref, sbuf_ref, craw_ref, pbuf_ref, *, tt):
    i = pl.program_id(1)

    @pl.when(i == 0)
    def _():
        zbuf_ref[pl.ds(0, V7X_SUBLANES), :] = jnp.zeros((V7X_SUBLANES, A_END), F32)
        cbuf_ref[pl.ds(0, CONV_HIST), :] = jnp.zeros((CONV_HIST, B_WIDTH), F32)
        pbuf_ref[pl.ds(0, POOL_HIST), :] = jnp.zeros((POOL_HIST, C_WIDTH), F32)

    n = _rms(h_ref[0], ng_ref[...]).astype(BF16)

    za = jnp.dot(n, w_ref[:, 0:A_END], preferred_element_type=F32)
    zbuf_ref[pl.ds(V7X_SUBLANES, tt), :] = za
    prev = zbuf_ref[pl.ds(V7X_SUBLANES - 1, tt), :]
    zbuf_ref[pl.ds(V7X_SUBLANES - 1, 1), :] = zbuf_ref[pl.ds(V7X_SUBLANES - 1 + tt, 1), :]
    za = za + (prev - za) * mu_ref[...]
    r = za[:, R0:R0 + A_WIDTH]
    k = za[:, K0:K0 + A_WIDTH]
    v = za[:, V0:V0 + A_WIDTH]
    w_log = -_softplus(-(w0_ref[...] + _bdot(jnp.tanh(za[:, WD0:AD0]), w2_ref[...]))) - 0.5
    dec_ref[0] = jnp.exp(-jnp.exp(w_log))
    a = jax.nn.sigmoid(a0_ref[...] + _bdot(za[:, AD0:GD0], a2_ref[...]))
    g_ref[0] = _bdot(jax.nn.sigmoid(za[:, GD0:A_END]), g2_ref[...])
    kk = k * kk_ref[...]
    kk = kk / jnp.maximum(jnp.sqrt(_head_sum(kk * kk, ones_ref)), 1e-12)
    k2 = k * (1.0 + (a - 1.0) * ka_ref[...])
    na_ref[0] = -kk
    bb_ref[0] = kk * a
    k2_ref[0] = k2
    r_ref[0] = r
    v_ref[0] = v
    bonus_ref[0] = _head_sum(r * k2 * rk_ref[...], ones_ref) * v

    glu = (jnp.dot(n, w_ref[:, BL0:BG0], preferred_element_type=F32)
           * jax.nn.sigmoid(jnp.dot(n, w_ref[:, BG0:C0], preferred_element_type=F32)))
    cbuf_ref[pl.ds(CONV_HIST, tt), :] = glu
    span = tt + CONV_HIST - V7X_SUBLANES
    for s in range(1, V7X_SUBLANES):
        sbuf_ref[s - 1] = cbuf_ref[pl.ds(s, span), :]
    first = CONV_HIST - (CONV_WIDTH - 1)

    def conv_rows(c, carry):
        base = pl.multiple_of(c * CONV_ROWS, CONV_ROWS)
        parts = [None] * CONV_PARTS
        for j in range(CONV_WIDTH):
            q, s = divmod(first + j, V7X_SUBLANES)
            rows = pl.ds(base + q * V7X_SUBLANES, CONV_ROWS)
            tap = cbuf_ref[rows, :] if s == 0 else sbuf_ref[s - 1, rows, :]
            term = cw_ref[pl.ds(j, 1), :] * tap
            parts[j % CONV_PARTS] = term if parts[j % CONV_PARTS] is None else parts[j % CONV_PARTS] + term
        craw_ref[pl.ds(base, CONV_ROWS), :] = functools.reduce(lambda p, q: p + q, parts)
        return carry

    lax.fori_loop(0, tt // CONV_ROWS, conv_rows, 0)
    cbuf_ref[pl.ds(0, CONV_HIST), :] = cbuf_ref[pl.ds(tt, CONV_HIST), :]
    conv = craw_ref[...] + cb_ref[...]
    cen = conv - jnp.mean(conv, -1, keepdims=True)
    var = jnp.mean(cen * cen, -1, keepdims=True)
    yb_ref[0] = _silu(cen * lax.rsqrt(var + LN_EPS) * lg_ref[...] + lb_ref[...])

    pbuf_ref[pl.ds(POOL_HIST, tt), :] = jnp.dot(n, w_ref[:, C0:P_END], preferred_element_type=F32)
    pos = (lax.broadcasted_iota(jnp.int32, (tt, C_GROUP_DIM), 0) + (i * tt + 1)).astype(F32)
    for gi, win in enumerate(POOL_WINDOWS):
        lanes = pl.ds(gi * C_GROUP_DIM, C_GROUP_DIM)
        cur = pbuf_ref[pl.ds(POOL_HIST, tt), lanes]
        tot = cur
        for d in range(1, win):
            tot = tot + pbuf_ref[pl.ds(POOL_HIST - d, tt), lanes]
        pooled = tot / jnp.minimum(pos, float(win)) - cur
        yc_ref[0, :, lanes] = _bdot(pooled, pw_ref[gi]) * ps_ref[:, lanes]
    pbuf_ref[pl.ds(0, POOL_HIST), :] = pbuf_ref[pl.ds(tt, POOL_HIST), :]


def mixer_prologue(h, bn, L, p):
    d = h.shape[-1]
    tt = _pick(L, (304, 144, 48))
    assert tt % CONV_ROWS == 0
    const = lambda a: pl.BlockSpec(a.shape, lambda b, i: (0,) * a.ndim)
    consts = [p["norm_g"], p["w_p"], p["mu"], p["w0"], p["w2"], p["a0"], p["a2"], p["g2"], p["k_k"], p["k_a"],
              p["r_k"], p["conv_w"], p["conv_b"], p["ln_g"], p["ln_b"], p["pool_w"], p["pool_s"], p["ones"]]
    out_spec = pl.BlockSpec((1, tt, A_WIDTH), lambda b, i: (b, i, 0))
    n_out = 10
    return pl.pallas_call(
        functools.partial(_prologue_kernel, tt=tt),
        grid=(bn, L // tt),
        in_specs=[pl.BlockSpec((1, tt, d), lambda b, i: (b, i, 0))] + [const(a) for a in consts],
        out_specs=[out_spec] * n_out,
        out_shape=[jax.ShapeDtypeStruct((bn, L, A_WIDTH), F32)] * n_out,
        scratch_shapes=[
            pltpu.VMEM((tt + V7X_SUBLANES, A_END), F32),
            pltpu.VMEM((tt + CONV_HIST, B_WIDTH), F32),
            pltpu.VMEM((V7X_SUBLANES - 1, tt + CONV_HIST - V7X_SUBLANES, B_WIDTH), F32),
            pltpu.VMEM((tt, B_WIDTH), F32),
            pltpu.VMEM((tt + POOL_HIST, C_WIDTH), F32),
        ],
        compiler_params=_params("parallel", "arbitrary"),
        name="mixer_prologue",
    )(h.reshape(bn, L, d), *consts)


def _sublane_total(x):
    p = jnp.sum(x, axis=0)
    p = p + pltpu.roll(p, 4, 0)
    p = p + pltpu.roll(p, 2, 0)
    return p + pltpu.roll(p, 1, 0)


def _wkv_kernel(a_ref, w_ref, b_ref, k_ref, r_ref, v_ref, y_ref, s_ref, x_ref, *, tt, nv):
    @pl.when(pl.program_id(0) == 0)
    def _():
        s_ref[...] = jnp.zeros_like(s_ref)

    half = V7X_LANES // 2
    rows = tt * (A_HEAD_DIM // 2)
    low_lanes = lax.broadcasted_iota(jnp.int32, (rows, V7X_LANES), 1) < half
    for c, ref in enumerate((a_ref, w_ref, b_ref, k_ref, r_ref)):
        x = ref[...].reshape(rows, V7X_LANES)
        swapped = pltpu.roll(x, half, 1)
        x_ref[c, :, 0:4] = jnp.where(low_lanes, x, swapped).reshape(tt, 4, 8, V7X_LANES)
        x_ref[c, :, 4:8] = jnp.where(low_lanes, swapped, x).reshape(tt, 4, 8, V7X_LANES)

    def step(t, carry):
        a = x_ref[0, t]
        w = x_ref[1, t]
        b = x_ref[2, t]
        k = x_ref[3, t]
        r = x_ref[4, t]
        wr = w * r
        br = _sublane_total(b * r)
        kr = _sublane_total(k * r)
        for vi in range(nv):
            s = s_ref[vi]
            vrow = v_ref[t, pl.ds(vi, 1), :]
            sa = _sublane_total(s * a)
            y0 = _sublane_total(s * wr)
            s_ref[vi] = s * w + sa[None] * b + vrow[None] * k
            y = y0 + sa * br + vrow * kr
            y_ref[t, pl.ds(vi, 1), :] = y[0:1]
        return carry

    lax.fori_loop(0, tt, step, 0)


def wkv_scan(a_s, w_s, b_s, k_s, r_s, v_s):
    L = a_s.shape[0]
    nv = v_s.shape[1]
    tt = _pick(L, (48, 16, 8, 1))
    row = pl.BlockSpec((tt, nv, V7X_LANES), lambda i: (i, 0, 0))
    return pl.pallas_call(
        functools.partial(_wkv_kernel, tt=tt, nv=nv),
        grid=(L // tt,),
        in_specs=[row] * 6,
        out_specs=row,
        out_shape=jax.ShapeDtypeStruct((L, nv, V7X_LANES), F32),
        scratch_shapes=[pltpu.VMEM((nv, 8, 8, V7X_LANES), F32), pltpu.VMEM((5, tt, 8, 8, V7X_LANES), F32)],
        compiler_params=_params("arbitrary"),
        name="wkv_scan",
    )(a_s, w_s, b_s, k_s, r_s, v_s)


def wkv_natural(na, dec, bb, k2, r, v, bn, L):
    half = A_HEAD_DIM // 2

    def to_lanes(x):
        return x.reshape(bn, L, A_HEADS, 2, half).transpose(1, 4, 3, 0, 2).reshape(L, half, 2 * bn * A_HEADS)

    y_s = wkv_scan(to_lanes(na), to_lanes(dec), to_lanes(bb), to_lanes(k2), to_lanes(r), to_lanes(v))
    return y_s.reshape(L, half, 2, bn, A_HEADS).transpose(3, 0, 4, 2, 1).reshape(bn * L, A_WIDTH)


def _merge_kernel(y_ref, bonus_ref, g_ref, yb_ref, yc_ref, h_ref, ng_ref, wg_ref, gb_ref, lw_ref, lb_ref,
                  pa_ref, pb_ref, pc_ref, wo_ref, ones_ref, *rest, route):
    if route:
        fg_ref, rwh_ref, rwl_ref, rb_ref, o_ref, n_ref, l_ref = rest
    else:
        (o_ref,) = rest
    d = h_ref.shape[-1]
    h = h_ref[...]
    gates = jax.nn.sigmoid(jnp.dot(_rms(h, ng_ref[...]).astype(BF16), wg_ref[...], preferred_element_type=F32)
                           + gb_ref[...])
    y = y_ref[...]
    cen = y - _head_sum(y, ones_ref) * (1.0 / A_HEAD_DIM)
    var = _head_sum(cen * cen, ones_ref) * (1.0 / A_HEAD_DIM)
    ya = (cen * lax.rsqrt(var + LNX_EPS) * lw_ref[...] + lb_ref[...] + bonus_ref[...]) * g_ref[...]
    merged = (gates[:, :d] * _bdot(ya, pa_ref[...]) + gates[:, d:2 * d] * _bdot(yb_ref[...], pb_ref[...])
              + gates[:, 2 * d:] * _bdot(yc_ref[...], pc_ref[...]))
    out = h + _bdot(merged, wo_ref[...])
    o_ref[...] = out
    if route:
        n = _rms(out, fg_ref[...])
        n_ref[...] = n
        n_hi = n.astype(BF16)
        n_lo = (n - n_hi.astype(F32)).astype(BF16)
        dot = lambda a, b_ref: jnp.dot(a, b_ref[...], preferred_element_type=F32)
        l_ref[...] = dot(n_hi, rwh_ref) + dot(n_lo, rwh_ref) + dot(n_hi, rwl_ref) + rb_ref[...]


def mixer_merge(y, bonus, g, y_b, y_c, h, p, route):
    m, d = h.shape
    tm = _pick(m, (384, 128, 8))
    rows = lambda width: pl.BlockSpec((tm, width), lambda i: (i, 0))
    const = lambda a: pl.BlockSpec(a.shape, lambda i: (0,) * a.ndim)
    consts = [p["norm_g"], p["w_g"], p["gate_b"], p["lnx_w"], p["lnx_b"], p["proj_a"], p["proj_b"], p["proj_c"],
              p["w_out"], p["ones"]]
    out_specs = [rows(d)]
    out_shape = [jax.ShapeDtypeStruct((m, d), F32)]
    if route:
        consts += [p["ffn_g"], p["router_w_hi"], p["router_w_lo"], p["router_b"]]
        out_specs += [rows(d), rows(ROUTE_LANES)]
        out_shape += [jax.ShapeDtypeStruct((m, d), F32), jax.ShapeDtypeStruct((m, ROUTE_LANES), F32)]
    return pl.pallas_call(
        functools.partial(_merge_kernel, route=route),
        grid=(m // tm,),
        in_specs=[rows(A_WIDTH)] * 3 + [rows(B_WIDTH), rows(C_WIDTH), rows(d)] + [const(a) for a in consts],
        out_specs=out_specs,
        out_shape=out_shape,
        compiler_params=_params("parallel"),
        name="mixer_merge",
    )(y, bonus, g, y_b, y_c, h, *consts)


def _ffn_kernel(h_ref, g_ref, fg_ref, wg_ref, wu_ref, wd_ref, o_ref, xn_ref, acc_ref, *, final):
    j = pl.program_id(1)

    @pl.when(j == 0)
    def _():
        xn_ref[...] = _rms(h_ref[...], g_ref[...]).astype(BF16)
        acc_ref[...] = jnp.zeros_like(acc_ref)

    x = xn_ref[...]
    gate = jnp.dot(x, wg_ref[...], preferred_element_type=F32)
    up = jnp.dot(x, wu_ref[...], preferred_element_type=F32)
    acc_ref[...] += _bdot(_silu(gate) * up, wd_ref[...])

    @pl.when(j == pl.num_programs(1) - 1)
    def _():
        hh = h_ref[...] + acc_ref[...]
        o_ref[...] = _rms(hh, fg_ref[...]) if final else hh


def dense_ffn(h, g, w_gate, w_up, w_down, final_g, final):
    m, d = h.shape
    ff = w_gate.shape[1]
    tm = _pick(m, (576, 384, 128, 8))
    tf = _pick(ff, (1408, 256, 128))
    return pl.pallas_call(
        functools.partial(_ffn_kernel, final=final),
        grid=(m // tm, ff // tf),
        in_specs=[
            pl.BlockSpec((tm, d), lambda i, j: (i, 0)),
            pl.BlockSpec((1, d), lambda i, j: (0, 0)),
            pl.BlockSpec((1, d), lambda i, j: (0, 0)),
            pl.BlockSpec((d, tf), lambda i, j: (0, j)),
            pl.BlockSpec((d, tf), lambda i, j: (0, j)),
            pl.BlockSpec((tf, d), lambda i, j: (j, 0)),
        ],
        out_specs=pl.BlockSpec((tm, d), lambda i, j: (i, 0)),
        out_shape=jax.ShapeDtypeStruct((m, d), F32),
        scratch_shapes=[pltpu.VMEM((tm, d), BF16), pltpu.VMEM((tm, d), F32)],
        compiler_params=_params("parallel", "arbitrary"),
        name="dense_ffn",
    )(h, g.reshape(1, d), final_g.reshape(1, d), w_gate, w_up, w_down)


def _row_copy(idx_ref, src_hbm, dst_ref, sem, r):
    return pltpu.make_async_copy(src_hbm.at[pl.ds(idx_ref[0, 0, r], 1)], dst_ref.at[pl.ds(r, 1)], sem)


def _start_row_gather(idx_ref, src_hbm, dst_ref, sem, n_rows):
    def start(r, c):
        _row_copy(idx_ref, src_hbm, dst_ref, sem, r).start()
        return c

    lax.fori_loop(0, n_rows, start, 0, unroll=8)


def _wait_row_gather(idx_ref, src_hbm, dst_ref, sem, n_rows):
    def wait(r, c):
        _row_copy(idx_ref, src_hbm, dst_ref, sem, r).wait()
        return c

    lax.fori_loop(0, n_rows, wait, 0, unroll=8)


def _moe_ffn_kernel(be_ref, tok_ref, nxt_ref, x_hbm, wg_ref, wu_ref, wd_ref, o_ref,
                    xraw_ref, x_ref, acc_ref, sem, *, rows_per_step):
    del be_ref
    i = pl.program_id(0)
    j = pl.program_id(1)
    slot = i % 2
    last_j = pl.num_programs(1) - 1

    @pl.when(j == 0)
    def _():
        @pl.when(i == 0)
        def _():
            _start_row_gather(tok_ref, x_hbm, xraw_ref.at[0], sem.at[0], MOE_ROWS)

        _wait_row_gather(tok_ref, x_hbm, xraw_ref.at[slot], sem.at[slot], MOE_ROWS)
        x_ref[...] = xraw_ref[slot].astype(BF16)
        acc_ref[...] = jnp.zeros_like(acc_ref)

    nxt_buf = xraw_ref.at[1 - slot]
    nxt_sem = sem.at[1 - slot]
    for r in range(rows_per_step):
        _row_copy(nxt_ref, x_hbm, nxt_buf, nxt_sem, j * rows_per_step + r).start()

    x = x_ref[...]
    gate = jnp.dot(x, wg_ref[0], preferred_element_type=F32)
    up = jnp.dot(x, wu_ref[0], preferred_element_type=F32)
    acc_ref[...] += _bdot(_silu(gate) * up, wd_ref[0])

    @pl.when(j == last_j)
    def _():
        o_ref[...] = acc_ref[...]

    @pl.when((i == pl.num_programs(0) - 1) & (j == last_j))
    def _():
        _wait_row_gather(nxt_ref, x_hbm, nxt_buf, nxt_sem, MOE_ROWS)


def moe_expert_ffn(n, block_e, slot_tok, e_gate, e_up, e_down):
    d = n.shape[1]
    ff = e_gate.shape[2]
    nb = block_e.shape[0]
    tf = _pick(ff, (1408, 256, 128))
    tok = slot_tok.reshape(nb, 1, MOE_ROWS)
    assert MOE_ROWS % (ff // tf) == 0
    grid_spec = pltpu.PrefetchScalarGridSpec(
        num_scalar_prefetch=1,
        grid=(nb, ff // tf),
        in_specs=[
            pl.BlockSpec((1, 1, MOE_ROWS), lambda i, j, be: (i, 0, 0), memory_space=pltpu.SMEM),
            pl.BlockSpec((1, 1, MOE_ROWS), lambda i, j, be: (jnp.minimum(i + 1, nb - 1), 0, 0),
                         memory_space=pltpu.SMEM),
            pl.BlockSpec(memory_space=pl.ANY),
            pl.BlockSpec((1, d, tf), lambda i, j, be: (be[i], 0, j)),
            pl.BlockSpec((1, d, tf), lambda i, j, be: (be[i], 0, j)),
            pl.BlockSpec((1, tf, d), lambda i, j, be: (be[i], j, 0)),
        ],
        out_specs=pl.BlockSpec((MOE_ROWS, d), lambda i, j, be: (i, 0)),
        scratch_shapes=[pltpu.VMEM((2, MOE_ROWS, d), F32), pltpu.VMEM((MOE_ROWS, d), BF16),
                        pltpu.VMEM((MOE_ROWS, d), F32), pltpu.SemaphoreType.DMA((2,))],
    )
    return pl.pallas_call(
        functools.partial(_moe_ffn_kernel, rows_per_step=MOE_ROWS // (ff // tf)),
        grid_spec=grid_spec,
        out_shape=jax.ShapeDtypeStruct((nb * MOE_ROWS, d), F32),
        compiler_params=_params("arbitrary", "arbitrary"),
        name="moe_expert_ffn",
    )(block_e, tok, tok, n, e_gate, e_up, e_down)


def _combine_kernel(d_ref, h_ref, gate_ref, g_ref, y_hbm, o_ref, buf_ref, sem, *, tm, final):
    _start_row_gather(d_ref, y_hbm, buf_ref, sem, TOP_K * tm)
    _wait_row_gather(d_ref, y_hbm, buf_ref, sem, TOP_K * tm)
    hh = h_ref[...]
    for k in range(TOP_K):
        hh = hh + buf_ref[pl.ds(k * tm, tm), :] * gate_ref[:, k:k + 1]
    o_ref[...] = _rms(hh, g_ref[...]) if final else hh


def moe_combine(h, y_slots, dest, gates, g, final):
    m, d = h.shape
    tm = _pick(m, (384, 128, 8))
    nt = m // tm
    idx = dest.reshape(nt, tm, TOP_K).transpose(0, 2, 1).reshape(nt, 1, TOP_K * tm)
    return pl.pallas_call(
        functools.partial(_combine_kernel, tm=tm, final=final),
        grid=(nt,),
        in_specs=[
            pl.BlockSpec((1, 1, TOP_K * tm), lambda i: (i, 0, 0), memory_space=pltpu.SMEM),
            pl.BlockSpec((tm, d), lambda i: (i, 0)),
            pl.BlockSpec((tm, TOP_K), lambda i: (i, 0)),
            pl.BlockSpec((1, d), lambda i: (0, 0)),
            pl.BlockSpec(memory_space=pl.ANY),
        ],
        out_specs=pl.BlockSpec((tm, d), lambda i: (i, 0)),
        out_shape=jax.ShapeDtypeStruct((m, d), F32),
        scratch_shapes=[pltpu.VMEM((TOP_K * tm, d), F32), pltpu.SemaphoreType.DMA(())],
        compiler_params=_params("arbitrary"),
        name="moe_combine",
    )(idx, h, gates, g.reshape(1, d), y_slots)


def _route(logits):
    t = logits.shape[0]
    top_l, top_e = lax.top_k(logits, TOP_K)
    gates = jax.nn.softmax(top_l, -1)
    flat_e = top_e.reshape(-1).astype(jnp.int32)
    onehot = (flat_e[:, None] == jnp.arange(N_EXPERTS, dtype=jnp.int32)[None, :]).astype(jnp.int32)
    csum = jnp.cumsum(onehot, axis=0)
    rank = jnp.sum(csum * onehot, axis=1) - 1
    counts = csum[-1]
    padded = (counts + MOE_ROWS - 1) // MOE_ROWS * MOE_ROWS
    pad_end = jnp.cumsum(padded)
    dest = (pad_end - padded)[flat_e] + rank
    nb = (t * TOP_K + N_EXPERTS * (MOE_ROWS - 1) + MOE_ROWS - 1) // MOE_ROWS
    slot_tok = jnp.zeros((nb * MOE_ROWS,), jnp.int32).at[dest].set(jnp.arange(t * TOP_K, dtype=jnp.int32) // TOP_K)
    block_e = jnp.minimum(jnp.searchsorted(pad_end, jnp.arange(nb, dtype=jnp.int32) * MOE_ROWS, side='right'),
                          N_EXPERTS - 1).astype(jnp.int32)
    return block_e, slot_tok, gates, dest.reshape(t, TOP_K).astype(jnp.int32)


def _layer_params(i, d, mix_norm_g, w_in, gate_b, a_mu, a_w0, a_w2, a_a0, a_a2, a_g2, a_kk, a_ka, a_rk, a_lnx_w,
                  a_lnx_b, b_conv_w, b_conv_b, b_ln_g, b_ln_b, c_pool_w, c_scale, proj_a, proj_b, proj_c, w_out):
    bf = lambda w: w.astype(BF16)
    row = lambda a: a.reshape(1, -1)
    s_a, s_b, s_c = A_IN, A_IN + 2 * B_WIDTH, A_IN + 2 * B_WIDTH + C_WIDTH

    def lane_pad(a, axis):
        o = 3 * A_WIDTH
        take = lambda lo, hi: lax.slice_in_dim(a, lo, hi, axis=axis)
        zshape = list(a.shape)
        zshape[axis] = V7X_LANES - W_LORA
        z = jnp.zeros(zshape, a.dtype)
        return jnp.concatenate([take(0, o), take(o, o + W_LORA), z, take(o + W_LORA, o + W_LORA + A_LORA), z,
                                take(o + W_LORA + A_LORA, s_a)], axis)

    w = w_in[i]
    w_p = jnp.concatenate([lane_pad(w[:, :s_a], 1), w[:, s_a:s_c]], 1)
    pad_rows = lambda a: jnp.concatenate([a, jnp.zeros((V7X_LANES - a.shape[0], a.shape[1]), a.dtype)], 0)
    head = jnp.arange(A_WIDTH, dtype=jnp.int32) // A_HEAD_DIM
    return {
        "norm_g": row(mix_norm_g[i]), "w_p": bf(w_p), "mu": row(lane_pad(a_mu[i], 0)),
        "w0": row(a_w0[i]), "w2": bf(pad_rows(a_w2[i])), "a0": row(a_a0[i]), "a2": bf(pad_rows(a_a2[i])),
        "g2": bf(a_g2[i]), "k_k": row(a_kk[i]), "k_a": row(a_ka[i]), "r_k": row(a_rk[i]),
        "conv_w": b_conv_w[i].reshape(CONV_WIDTH, B_WIDTH), "conv_b": row(b_conv_b[i]),
        "ln_g": row(b_ln_g[i]), "ln_b": row(b_ln_b[i]), "pool_w": bf(c_pool_w[i]), "pool_s": row(c_scale[i]),
        "ones": (head[:, None] == head[None, :]).astype(BF16),
        "w_g": bf(w[:, s_c:]), "gate_b": row(gate_b[i]), "lnx_w": row(a_lnx_w[i]), "lnx_b": row(a_lnx_b[i]),
        "proj_a": bf(proj_a[i]), "proj_b": bf(proj_b[i]), "proj_c": bf(proj_c[i]), "w_out": bf(w_out[i]),
    }


def kernel(x, meta_tokens, mix_norm_g, w_in, gate_b, a_mu, a_w0, a_w2, a_a0, a_a2, a_g2, a_kk, a_ka, a_rk, a_lnx_w, a_lnx_b, b_conv_w, b_conv_b, b_ln_g, b_ln_b, c_pool_w, c_scale, proj_a, proj_b, proj_c, w_out, ffn_norm_g, ffn_gate, ffn_up, ffn_down, router_w, router_b, exp_gate, exp_up, exp_down, final_norm_g):
    bn, seq, d = x.shape
    depth = w_in.shape[0]
    L = N_META + seq
    assert 2 * bn * A_HEADS == V7X_LANES, "the recurrence kernel maps (value half, batch, head) onto the 128 lanes"
    bf = lambda w: w.astype(BF16)
    meta = jnp.broadcast_to(meta_tokens[None].astype(x.dtype), (bn, N_META, d))
    h = jnp.concatenate([meta, x], 1).reshape(bn * L, d)
    for i in range(depth):
        p = _layer_params(i, d, mix_norm_g, w_in, gate_b, a_mu, a_w0, a_w2, a_a0, a_a2, a_g2, a_kk, a_ka, a_rk,
                          a_lnx_w, a_lnx_b, b_conv_w, b_conv_b, b_ln_g, b_ln_b, c_pool_w, c_scale,
                          proj_a, proj_b, proj_c, w_out)
        last = i == depth - 1
        j = i // 2
        moe = i % 2 == 1
        if moe:
            p["ffn_g"] = ffn_norm_g[i].reshape(1, d)
            rw = jnp.zeros((d, ROUTE_LANES), F32).at[:, :N_EXPERTS].set(router_w[j])
            p["router_w_hi"] = bf(rw)
            p["router_w_lo"] = bf(rw - bf(rw).astype(F32))
            p["router_b"] = jnp.zeros((1, ROUTE_LANES), F32).at[0, :N_EXPERTS].set(router_b[j])
        na, dec, bb, k2, r, v, bonus, g, y_b, y_c = mixer_prologue(h, bn, L, p)
        y = wkv_natural(na, dec, bb, k2, r, v, bn, L)
        flat = lambda a: a.reshape(bn * L, a.shape[-1])
        merged = mixer_merge(y, flat(bonus), flat(g), flat(y_b), flat(y_c), h, p, moe)
        if moe:
            h, n, logits = merged
            block_e, slot_tok, gates, dest = _route(logits[:, :N_EXPERTS])
            y_slots = moe_expert_ffn(n, block_e, slot_tok, bf(exp_gate[j]), bf(exp_up[j]), bf(exp_down[j]))
            h = moe_combine(h, y_slots, dest, gates, final_norm_g, last)
        else:
            (h,) = merged
            h = dense_ffn(h, ffn_norm_g[i], bf(ffn_gate[j]), bf(ffn_up[j]), bf(ffn_down[j]), final_norm_g, last)
    return h.reshape(bn, L, d)[:, N_META:]
```

```python
import functools

import jax
import jax.numpy as jnp
from jax import lax
from jax.experimental import pallas as pl
from jax.experimental.pallas import tpu as pltpu

F32 = jnp.float32
BF16 = jnp.bfloat16

N_META = 16
A_HEADS = 8
A_HEAD_DIM = 64
A_WIDTH = A_HEADS * A_HEAD_DIM
W_LORA = 64
A_LORA = 64
G_LORA = 128
A_IN = 3 * A_WIDTH + W_LORA + A_LORA + G_LORA
LNX_EPS = 64e-5
B_WIDTH = 512
CONV_WIDTH = 31
C_WIDTH = 512
POOL_WINDOWS = (2, 4, 8, 16)
C_GROUP_DIM = C_WIDTH // len(POOL_WINDOWS)
N_EXPERTS = 8
TOP_K = 2
RMS_EPS = 1e-6
LN_EPS = 1e-5

V7X_LANES = 128
V7X_SUBLANES = 8
VMEM_LIMIT_BYTES = 58 * 1024 * 1024
ROUTE_LANES = V7X_LANES
MOE_ROWS = 512
CONV_HIST = 32
POOL_HIST = 16
CONV_ROWS = 16
CONV_PARTS = 4

R0, K0, V0 = 0, A_WIDTH, 2 * A_WIDTH
WD0 = 3 * A_WIDTH
AD0 = WD0 + V7X_LANES
GD0 = AD0 + V7X_LANES
A_END = GD0 + G_LORA
BL0 = A_END
BG0 = BL0 + B_WIDTH
C0 = BG0 + B_WIDTH
P_END = C0 + C_WIDTH


def _pick(n, candidates):
    for c in candidates:
        if n % c == 0:
            return c
    raise ValueError(f"no tile in {candidates} divides {n}")


def _params(*sem):
    return pltpu.CompilerParams(dimension_semantics=sem, vmem_limit_bytes=VMEM_LIMIT_BYTES)


def _rms(x, g):
    return x * lax.rsqrt(jnp.mean(x * x, -1, keepdims=True) + RMS_EPS) * g


def _silu(x):
    return x * jax.nn.sigmoid(x)


def _softplus(x):
    return jnp.maximum(x, 0.0) + jnp.log(1.0 + jnp.exp(-jnp.abs(x)))


def _bdot(x, w):
    return jnp.dot(x.astype(BF16), w, preferred_element_type=F32)


def _head_sum(x, ones_ref):
    hi = x.astype(BF16)
    lo = (x - hi.astype(F32)).astype(BF16)
    ones = ones_ref[...]
    return jnp.dot(hi, ones, preferred_element_type=F32) + jnp.dot(lo, ones, preferred_element_type=F32)


def _prologue_kernel(h_ref, ng_ref, w_ref, mu_ref, w0_ref, w2_ref, a0_ref, a2_ref, g2_ref, kk_ref, ka_ref, rk_ref,
                     cw_ref, cb_ref, lg_ref, lb_ref, pw_ref, ps_ref, ones_ref,
                     na_ref, dec_ref, bb_ref, k2_ref, r_ref, v_ref, bonus_ref, g_ref, yb_ref, yc_ref,
                     zbuf_ref, cbuf_ref, sbuf_ref, craw_ref, pbuf_ref, *, tt):
    i = pl.program_id(1)

    @pl.when(i == 0)
    def _():
        zbuf_ref[pl.ds(0, V7X_SUBLANES), :] = jnp.zeros((V7X_SUBLANES, A_END), F32)
        cbuf_ref[pl.ds(0, CONV_HIST), :] = jnp.zeros((CONV_HIST, B_WIDTH), F32)
        pbuf_ref[pl.ds(0, POOL_HIST), :] = jnp.zeros((POOL_HIST, C_WIDTH), F32)

    n = _rms(h_ref[0], ng_ref[...]).astype(BF16)

    za = jnp.dot(n, w_ref[:, 0:A_END], preferred_element_type=F32)
    zbuf_ref[pl.ds(V7X_SUBLANES, tt), :] = za
    prev = zbuf_ref[pl.ds(V7X_SUBLANES - 1, tt), :]
    zbuf_ref[pl.ds(V7X_SUBLANES - 1, 1), :] = zbuf_ref[pl.ds(V7X_SUBLANES - 1 + tt, 1), :]
    za = za + (prev - za) * mu_ref[...]
    r = za[:, R0:R0 + A_WIDTH]
    k = za[:, K0:K0 + A_WIDTH]
    v = za[:, V0:V0 + A_WIDTH]
    w_log = -_softplus(-(w0_ref[...] + _bdot(jnp.tanh(za[:, WD0:AD0]), w2_ref[...]))) - 0.5
    dec_ref[0] = jnp.exp(-jnp.exp(w_log))
    a = jax.nn.sigmoid(a0_ref[...] + _bdot(za[:, AD0:GD0], a2_ref[...]))
    g_ref[0] = _bdot(jax.nn.sigmoid(za[:, GD0:A_END]), g2_ref[...])
    kk = k * kk_ref[...]
    kk = kk / jnp.maximum(jnp.sqrt(_head_sum(kk * kk, ones_ref)), 1e-12)
    k2 = k * (1.0 + (a - 1.0) * ka_ref[...])
    na_ref[0] = (-kk).astype(BF16)
    bb_ref[0] = (kk * a).astype(BF16)
    k2_ref[0] = k2.astype(BF16)
    r_ref[0] = r.astype(BF16)
    v_ref[0] = v.astype(BF16)
    bonus_ref[0] = _head_sum(r * k2 * rk_ref[...], ones_ref) * v

    glu = (jnp.dot(n, w_ref[:, BL0:BG0], preferred_element_type=F32)
           * jax.nn.sigmoid(jnp.dot(n, w_ref[:, BG0:C0], preferred_element_type=F32)))
    cbuf_ref[pl.ds(CONV_HIST, tt), :] = glu
    span = tt + CONV_HIST - V7X_SUBLANES
    for s in range(1, V7X_SUBLANES):
        sbuf_ref[s - 1] = cbuf_ref[pl.ds(s, span), :]
    first = CONV_HIST - (CONV_WIDTH - 1)

    def conv_rows(c, carry):
        base = pl.multiple_of(c * CONV_ROWS, CONV_ROWS)
        parts = [None] * CONV_PARTS
        for j in range(CONV_WIDTH):
            q, s = divmod(first + j, V7X_SUBLANES)
            rows = pl.ds(base + q * V7X_SUBLANES, CONV_ROWS)
            tap = cbuf_ref[rows, :] if s == 0 else sbuf_ref[s - 1, rows, :]
            term = cw_ref[pl.ds(j, 1), :] * tap
            parts[j % CONV_PARTS] = term if parts[j % CONV_PARTS] is None else parts[j % CONV_PARTS] + term
        craw_ref[pl.ds(base, CONV_ROWS), :] = functools.reduce(lambda p, q: p + q, parts)
        return carry

    lax.fori_loop(0, tt // CONV_ROWS, conv_rows, 0)
    cbuf_ref[pl.ds(0, CONV_HIST), :] = cbuf_ref[pl.ds(tt, CONV_HIST), :]
    conv = craw_ref[...] + cb_ref[...]
    cen = conv - jnp.mean(conv, -1, keepdims=True)
    var = jnp.mean(cen * cen, -1, keepdims=True)
    yb_ref[0] = _silu(cen * lax.rsqrt(var + LN_EPS) * lg_ref[...] + lb_ref[...]).astype(BF16)

    pbuf_ref[pl.ds(POOL_HIST, tt), :] = jnp.dot(n, w_ref[:, C0:P_END], preferred_element_type=F32)
    pos = (lax.broadcasted_iota(jnp.int32, (tt, C_GROUP_DIM), 0) + (i * tt + 1)).astype(F32)
    for gi, win in enumerate(POOL_WINDOWS):
        lanes = pl.ds(gi * C_GROUP_DIM, C_GROUP_DIM)
        cur = pbuf_ref[pl.ds(POOL_HIST, tt), lanes]
        tot = cur
        for d in range(1, win):
            tot = tot + pbuf_ref[pl.ds(POOL_HIST - d, tt), lanes]
        pooled = tot / jnp.minimum(pos, float(win)) - cur
        yc_ref[0, :, lanes] = (_bdot(pooled, pw_ref[gi]) * ps_ref[:, lanes]).astype(BF16)
    pbuf_ref[pl.ds(0, POOL_HIST), :] = pbuf_ref[pl.ds(tt, POOL_HIST), :]


def mixer_prologue(h, bn, L, p):
    d = h.shape[-1]
    tt = _pick(L, (304, 144, 48))
    assert tt % CONV_ROWS == 0
    const = lambda a: pl.BlockSpec(a.shape, lambda b, i: (0,) * a.ndim)
    consts = [p["norm_g"], p["w_p"], p["mu"], p["w0"], p["w2"], p["a0"], p["a2"], p["g2"], p["k_k"], p["k_a"],
              p["r_k"], p["conv_w"], p["conv_b"], p["ln_g"], p["ln_b"], p["pool_w"], p["pool_s"], p["ones"]]
    out_spec = pl.BlockSpec((1, tt, A_WIDTH), lambda b, i: (b, i, 0))
    out_dtypes = [BF16, F32, BF16, BF16, BF16, BF16, F32, F32, BF16, BF16]
    return pl.pallas_call(
        functools.partial(_prologue_kernel, tt=tt),
        grid=(bn, L // tt),
        in_specs=[pl.BlockSpec((1, tt, d), lambda b, i: (b, i, 0))] + [const(a) for a in consts],
        out_specs=[out_spec] * len(out_dtypes),
        out_shape=[jax.ShapeDtypeStruct((bn, L, A_WIDTH), dt) for dt in out_dtypes],
        scratch_shapes=[
            pltpu.VMEM((tt + V7X_SUBLANES, A_END), F32),
            pltpu.VMEM((tt + CONV_HIST, B_WIDTH), F32),
            pltpu.VMEM((V7X_SUBLANES - 1, tt + CONV_HIST - V7X_SUBLANES, B_WIDTH), F32),
            pltpu.VMEM((tt, B_WIDTH), F32),
            pltpu.VMEM((tt + POOL_HIST, C_WIDTH), F32),
        ],
        compiler_params=_params("parallel", "arbitrary"),
        name="mixer_prologue",
    )(h.reshape(bn, L, d), *consts)


def _sublane_total(x):
    p = jnp.sum(x, axis=0)
    p = p + pltpu.roll(p, 4, 0)
    p = p + pltpu.roll(p, 2, 0)
    return p + pltpu.roll(p, 1, 0)


def _wkv_kernel(a_ref, w_ref, b_ref, k_ref, r_ref, v_ref, y_ref, s_ref, x_ref, vx_ref, *, tt, nv):
    @pl.when(pl.program_id(0) == 0)
    def _():
        s_ref[...] = jnp.zeros_like(s_ref)

    half = V7X_LANES // 2
    rows = tt * nv
    low_lanes = lax.broadcasted_iota(jnp.int32, (rows, V7X_LANES), 1) < half
    for c, ref in enumerate((a_ref, w_ref, b_ref, k_ref, r_ref)):
        x = ref[...].astype(F32).reshape(rows, V7X_LANES)
        swapped = pltpu.roll(x, half, 1)
        x_ref[c, :, 0:4] = jnp.where(low_lanes, x, swapped).reshape(tt, 4, 8, V7X_LANES)
        x_ref[c, :, 4:8] = jnp.where(low_lanes, swapped, x).reshape(tt, 4, 8, V7X_LANES)
    vx_ref[...] = v_ref[...].astype(F32)

    def step(t, carry):
        a = x_ref[0, t]
        w = x_ref[1, t]
        b = x_ref[2, t]
        k = x_ref[3, t]
        r = x_ref[4, t]
        wr = w * r
        br = _sublane_total(b * r)
        kr = _sublane_total(k * r)
        for vi in range(nv):
            s = s_ref[vi]
            vrow = vx_ref[t, pl.ds(vi, 1), :]
            sa = _sublane_total(s * a)
            y0 = _sublane_total(s * wr)
            s_ref[vi] = s * w + sa[None] * b + vrow[None] * k
            y = y0 + sa * br + vrow * kr
            y_ref[t, pl.ds(vi, 1), :] = y[0:1]
        return carry

    lax.fori_loop(0, tt, step, 0)


def wkv_scan(a_s, w_s, b_s, k_s, r_s, v_s):
    L = a_s.shape[0]
    nv = v_s.shape[1]
    tt = _pick(L, (48, 16, 8, 1))
    row = pl.BlockSpec((tt, nv, V7X_LANES), lambda i: (i, 0, 0))
    return pl.pallas_call(
        functools.partial(_wkv_kernel, tt=tt, nv=nv),
        grid=(L // tt,),
        in_specs=[row] * 6,
        out_specs=row,
        out_shape=jax.ShapeDtypeStruct((L, nv, V7X_LANES), F32),
        scratch_shapes=[pltpu.VMEM((nv, 8, 8, V7X_LANES), F32), pltpu.VMEM((5, tt, 8, 8, V7X_LANES), F32),
                        pltpu.VMEM((tt, nv, V7X_LANES), F32)],
        compiler_params=_params("arbitrary"),
        name="wkv_scan",
    )(a_s, w_s, b_s, k_s, r_s, v_s)


def wkv_natural(na, dec, bb, k2, r, v, bn, L):
    half = A_HEAD_DIM // 2

    def to_lanes(x):
        return x.reshape(bn, L, A_HEADS, 2, half).transpose(1, 4, 3, 0, 2).reshape(L, half, 2 * bn * A_HEADS)

    y_s = wkv_scan(to_lanes(na), to_lanes(dec), to_lanes(bb), to_lanes(k2), to_lanes(r), to_lanes(v))
    return y_s.reshape(L, half, 2, bn, A_HEADS).transpose(3, 0, 4, 2, 1).reshape(bn * L, A_WIDTH)


def _merge_kernel(y_ref, bonus_ref, g_ref, yb_ref, yc_ref, h_ref, ng_ref, wg_ref, gb_ref, lw_ref, lb_ref,
                  pa_ref, pb_ref, pc_ref, wo_ref, ones_ref, *rest, route):
    if route:
        fg_ref, rwh_ref, rwl_ref, rb_ref, o_ref, n_ref, l_ref = rest
    else:
        (o_ref,) = rest
    d = h_ref.shape[-1]
    h = h_ref[...]
    gates = jax.nn.sigmoid(jnp.dot(_rms(h, ng_ref[...]).astype(BF16), wg_ref[...], preferred_element_type=F32)
                           + gb_ref[...])
    y = y_ref[...]
    cen = y - _head_sum(y, ones_ref) * (1.0 / A_HEAD_DIM)
    var = _head_sum(cen * cen, ones_ref) * (1.0 / A_HEAD_DIM)
    ya = (cen * lax.rsqrt(var + LNX_EPS) * lw_ref[...] + lb_ref[...] + bonus_ref[...]) * g_ref[...]
    merged = (gates[:, :d] * _bdot(ya, pa_ref[...]) + gates[:, d:2 * d] * _bdot(yb_ref[...], pb_ref[...])
              + gates[:, 2 * d:] * _bdot(yc_ref[...], pc_ref[...]))
    out = h + _bdot(merged, wo_ref[...])
    o_ref[...] = out
    if route:
        n = _rms(out, fg_ref[...])
        n_ref[...] = n
        n_hi = n.astype(BF16)
        n_lo = (n - n_hi.astype(F32)).astype(BF16)
        dot = lambda a, b_ref: jnp.dot(a, b_ref[...], preferred_element_type=F32)
        l_ref[...] = dot(n_hi, rwh_ref) + dot(n_lo, rwh_ref) + dot(n_hi, rwl_ref) + rb_ref[...]


def mixer_merge(y, bonus, g, y_b, y_c, h, p, route):
    m, d = h.shape
    tm = _pick(m, (384, 128, 8))
    rows = lambda width: pl.BlockSpec((tm, width), lambda i: (i, 0))
    const = lambda a: pl.BlockSpec(a.shape, lambda i: (0,) * a.ndim)
    consts = [p["norm_g"], p["w_g"], p["gate_b"], p["lnx_w"], p["lnx_b"], p["proj_a"], p["proj_b"], p["proj_c"],
              p["w_out"], p["ones"]]
    out_specs = [rows(d)]
    out_shape = [jax.ShapeDtypeStruct((m, d), F32)]
    if route:
        consts += [p["ffn_g"], p["router_w_hi"], p["router_w_lo"], p["router_b"]]
        out_specs += [rows(d), rows(ROUTE_LANES)]
        out_shape += [jax.ShapeDtypeStruct((m, d), F32), jax.ShapeDtypeStruct((m, ROUTE_LANES), F32)]
    return pl.pallas_call(
        functools.partial(_merge_kernel, route=route),
        grid=(m // tm,),
        in_specs=[rows(A_WIDTH)] * 3 + [rows(B_WIDTH), rows(C_WIDTH), rows(d)] + [const(a) for a in consts],
        out_specs=out_specs,
        out_shape=out_shape,
        compiler_params=_params("parallel"),
        name="mixer_merge",
    )(y, bonus, g, y_b, y_c, h, *consts)


def _ffn_kernel(h_ref, g_ref, fg_ref, wg_ref, wu_ref, wd_ref, o_ref, xn_ref, acc_ref, *, final):
    j = pl.program_id(1)

    @pl.when(j == 0)
    def _():
        xn_ref[...] = _rms(h_ref[...], g_ref[...]).astype(BF16)
        acc_ref[...] = jnp.zeros_like(acc_ref)

    x = xn_ref[...]
    gate = jnp.dot(x, wg_ref[...], preferred_element_type=F32)
    up = jnp.dot(x, wu_ref[...], preferred_element_type=F32)
    acc_ref[...] += _bdot(_silu(gate) * up, wd_ref[...])

    @pl.when(j == pl.num_programs(1) - 1)
    def _():
        hh = h_ref[...] + acc_ref[...]
        o_ref[...] = _rms(hh, fg_ref[...]) if final else hh


def dense_ffn(h, g, w_gate, w_up, w_down, final_g, final):
    m, d = h.shape
    ff = w_gate.shape[1]
    tm = _pick(m, (576, 384, 128, 8))
    tf = _pick(ff, (1408, 256, 128))
    return pl.pallas_call(
        functools.partial(_ffn_kernel, final=final),
        grid=(m // tm, ff // tf),
        in_specs=[
            pl.BlockSpec((tm, d), lambda i, j: (i, 0)),
            pl.BlockSpec((1, d), lambda i, j: (0, 0)),
            pl.BlockSpec((1, d), lambda i, j: (0, 0)),
            pl.BlockSpec((d, tf), lambda i, j: (0, j)),
            pl.BlockSpec((d, tf), lambda i, j: (0, j)),
            pl.BlockSpec((tf, d), lambda i, j: (j, 0)),
        ],
        out_specs=pl.BlockSpec((tm, d), lambda i, j: (i, 0)),
        out_shape=jax.ShapeDtypeStruct((m, d), F32),
        scratch_shapes=[pltpu.VMEM((tm, d), BF16), pltpu.VMEM((tm, d), F32)],
        compiler_params=_params("parallel", "arbitrary"),
        name="dense_ffn",
    )(h, g.reshape(1, d), final_g.reshape(1, d), w_gate, w_up, w_down)


def _row_copy(idx_ref, src_hbm, dst_ref, sem, r):
    return pltpu.make_async_copy(src_hbm.at[pl.ds(idx_ref[0, 0, r], 1)], dst_ref.at[pl.ds(r, 1)], sem)


def _start_row_gather(idx_ref, src_hbm, dst_ref, sem, n_rows):
    def start(r, c):
        _row_copy(idx_ref, src_hbm, dst_ref, sem, r).start()
        return c

    lax.fori_loop(0, n_rows, start, 0, unroll=8)


def _wait_row_gather(idx_ref, src_hbm, dst_ref, sem, n_rows):
    def wait(r, c):
        _row_copy(idx_ref, src_hbm, dst_ref, sem, r).wait()
        return c

    lax.fori_loop(0, n_rows, wait, 0, unroll=8)


def _moe_ffn_kernel(be_ref, tok_ref, nxt_ref, x_hbm, wg_ref, wu_ref, wd_ref, o_ref,
                    xraw_ref, x_ref, acc_ref, sem, *, rows_per_step):
    del be_ref
    i = pl.program_id(0)
    j = pl.program_id(1)
    slot = i % 2
    last_j = pl.num_programs(1) - 1

    @pl.when(j == 0)
    def _():
        @pl.when(i == 0)
        def _():
            _start_row_gather(tok_ref, x_hbm, xraw_ref.at[0], sem.at[0], MOE_ROWS)

        _wait_row_gather(tok_ref, x_hbm, xraw_ref.at[slot], sem.at[slot], MOE_ROWS)
        x_ref[...] = xraw_ref[slot].astype(BF16)
        acc_ref[...] = jnp.zeros_like(acc_ref)

    nxt_buf = xraw_ref.at[1 - slot]
    nxt_sem = sem.at[1 - slot]
    for r in range(rows_per_step):
        _row_copy(nxt_ref, x_hbm, nxt_buf, nxt_sem, j * rows_per_step + r).start()

    x = x_ref[...]
    gate = jnp.dot(x, wg_ref[0], preferred_element_type=F32)
    up = jnp.dot(x, wu_ref[0], preferred_element_type=F32)
    acc_ref[...] += _bdot(_silu(gate) * up, wd_ref[0])

    @pl.when(j == last_j)
    def _():
        o_ref[...] = acc_ref[...]

    @pl.when((i == pl.num_programs(0) - 1) & (j == last_j))
    def _():
        _wait_row_gather(nxt_ref, x_hbm, nxt_buf, nxt_sem, MOE_ROWS)


def moe_expert_ffn(n, block_e, slot_tok, e_gate, e_up, e_down):
    d = n.shape[1]
    ff = e_gate.shape[2]
    nb = block_e.shape[0]
    tf = _pick(ff, (1408, 256, 128))
    tok = slot_tok.reshape(nb, 1, MOE_ROWS)
    assert MOE_ROWS % (ff // tf) == 0
    grid_spec = pltpu.PrefetchScalarGridSpec(
        num_scalar_prefetch=1,
        grid=(nb, ff // tf),
        in_specs=[
            pl.BlockSpec((1, 1, MOE_ROWS), lambda i, j, be: (i, 0, 0), memory_space=pltpu.SMEM),
            pl.BlockSpec((1, 1, MOE_ROWS), lambda i, j, be: (jnp.minimum(i + 1, nb - 1), 0, 0),
                         memory_space=pltpu.SMEM),
            pl.BlockSpec(memory_space=pl.ANY),
            pl.BlockSpec((1, d, tf), lambda i, j, be: (be[i], 0, j)),
            pl.BlockSpec((1, d, tf), lambda i, j, be: (be[i], 0, j)),
            pl.BlockSpec((1, tf, d), lambda i, j, be: (be[i], j, 0)),
        ],
        out_specs=pl.BlockSpec((MOE_ROWS, d), lambda i, j, be: (i, 0)),
        scratch_shapes=[pltpu.VMEM((2, MOE_ROWS, d), F32), pltpu.VMEM((MOE_ROWS, d), BF16),
                        pltpu.VMEM((MOE_ROWS, d), F32), pltpu.SemaphoreType.DMA((2,))],
    )
    return pl.pallas_call(
        functools.partial(_moe_ffn_kernel, rows_per_step=MOE_ROWS // (ff // tf)),
        grid_spec=grid_spec,
        out_shape=jax.ShapeDtypeStruct((nb * MOE_ROWS, d), F32),
        compiler_params=_params("arbitrary", "arbitrary"),
        name="moe_expert_ffn",
    )(block_e, tok, tok, n, e_gate, e_up, e_down)


def _combine_kernel(d_ref, h_ref, gate_ref, g_ref, y_hbm, o_ref, buf_ref, sem, *, tm, final):
    _start_row_gather(d_ref, y_hbm, buf_ref, sem, TOP_K * tm)
    _wait_row_gather(d_ref, y_hbm, buf_ref, sem, TOP_K * tm)
    hh = h_ref[...]
    for k in range(TOP_K):
        hh = hh + buf_ref[pl.ds(k * tm, tm), :] * gate_ref[:, k:k + 1]
    o_ref[...] = _rms(hh, g_ref[...]) if final else hh


def moe_combine(h, y_slots, dest, gates, g, final):
    m, d = h.shape
    tm = _pick(m, (384, 128, 8))
    nt = m // tm
    idx = dest.reshape(nt, tm, TOP_K).transpose(0, 2, 1).reshape(nt, 1, TOP_K * tm)
    return pl.pallas_call(
        functools.partial(_combine_kernel, tm=tm, final=final),
        grid=(nt,),
        in_specs=[
            pl.BlockSpec((1, 1, TOP_K * tm), lambda i: (i, 0, 0), memory_space=pltpu.SMEM),
            pl.BlockSpec((tm, d), lambda i: (i, 0)),
            pl.BlockSpec((tm, TOP_K), lambda i: (i, 0)),
            pl.BlockSpec((1, d), lambda i: (0, 0)),
            pl.BlockSpec(memory_space=pl.ANY),
        ],
        out_specs=pl.BlockSpec((tm, d), lambda i: (i, 0)),
        out_shape=jax.ShapeDtypeStruct((m, d), F32),
        scratch_shapes=[pltpu.VMEM((TOP_K * tm, d), F32), pltpu.SemaphoreType.DMA(())],
        compiler_params=_params("arbitrary"),
        name="moe_combine",
    )(idx, h, gates, g.reshape(1, d), y_slots)


def _route(logits):
    t = logits.shape[0]
    top_l, top_e = lax.top_k(logits, TOP_K)
    gates = jax.nn.softmax(top_l, -1)
    flat_e = top_e.reshape(-1).astype(jnp.int32)
    onehot = (flat_e[:, None] == jnp.arange(N_EXPERTS, dtype=jnp.int32)[None, :]).astype(jnp.int32)
    csum = jnp.cumsum(onehot, axis=0)
    rank = jnp.sum(csum * onehot, axis=1) - 1
    counts = csum[-1]
    padded = (counts + MOE_ROWS - 1) // MOE_ROWS * MOE_ROWS
    pad_end = jnp.cumsum(padded)
    dest = (pad_end - padded)[flat_e] + rank
    nb = (t * TOP_K + N_EXPERTS * (MOE_ROWS - 1) + MOE_ROWS - 1) // MOE_ROWS
    slot_tok = jnp.zeros((nb * MOE_ROWS,), jnp.int32).at[dest].set(jnp.arange(t * TOP_K, dtype=jnp.int32) // TOP_K)
    block_e = jnp.minimum(jnp.searchsorted(pad_end, jnp.arange(nb, dtype=jnp.int32) * MOE_ROWS, side='right'),
                          N_EXPERTS - 1).astype(jnp.int32)
    return block_e, slot_tok, gates, dest.reshape(t, TOP_K).astype(jnp.int32)


def _layer_params(i, d, mix_norm_g, w_in, gate_b, a_mu, a_w0, a_w2, a_a0, a_a2, a_g2, a_kk, a_ka, a_rk, a_lnx_w,
                  a_lnx_b, b_conv_w, b_conv_b, b_ln_g, b_ln_b, c_pool_w, c_scale, proj_a, proj_b, proj_c, w_out):
    bf = lambda w: w.astype(BF16)
    row = lambda a: a.reshape(1, -1)
    s_a, s_b, s_c = A_IN, A_IN + 2 * B_WIDTH, A_IN + 2 * B_WIDTH + C_WIDTH

    def lane_pad(a, axis):
        o = 3 * A_WIDTH
        take = lambda lo, hi: lax.slice_in_dim(a, lo, hi, axis=axis)
        zshape = list(a.shape)
        zshape[axis] = V7X_LANES - W_LORA
        z = jnp.zeros(zshape, a.dtype)
        return jnp.concatenate([take(0, o), take(o, o + W_LORA), z, take(o + W_LORA, o + W_LORA + A_LORA), z,
                                take(o + W_LORA + A_LORA, s_a)], axis)

    w = w_in[i]
    w_p = jnp.concatenate([lane_pad(w[:, :s_a], 1), w[:, s_a:s_c]], 1)
    pad_rows = lambda a: jnp.concatenate([a, jnp.zeros((V7X_LANES - a.shape[0], a.shape[1]), a.dtype)], 0)
    head = jnp.arange(A_WIDTH, dtype=jnp.int32) // A_HEAD_DIM
    return {
        "norm_g": row(mix_norm_g[i]), "w_p": bf(w_p), "mu": row(lane_pad(a_mu[i], 0)),
        "w0": row(a_w0[i]), "w2": bf(pad_rows(a_w2[i])), "a0": row(a_a0[i]), "a2": bf(pad_rows(a_a2[i])),
        "g2": bf(a_g2[i]), "k_k": row(a_kk[i]), "k_a": row(a_ka[i]), "r_k": row(a_rk[i]),
        "conv_w": b_conv_w[i].reshape(CONV_WIDTH, B_WIDTH), "conv_b": row(b_conv_b[i]),
        "ln_g": row(b_ln_g[i]), "ln_b": row(b_ln_b[i]), "pool_w": bf(c_pool_w[i]), "pool_s": row(c_scale[i]),
        "ones": (head[:, None] == head[None, :]).astype(BF16),
        "w_g": bf(w[:, s_c:]), "gate_b": row(gate_b[i]), "lnx_w": row(a_lnx_w[i]), "lnx_b": row(a_lnx_b[i]),
        "proj_a": bf(proj_a[i]), "proj_b": bf(proj_b[i]), "proj_c": bf(proj_c[i]), "w_out": bf(w_out[i]),
    }


def kernel(x, meta_tokens, mix_norm_g, w_in, gate_b, a_mu, a_w0, a_w2, a_a0, a_a2, a_g2, a_kk, a_ka, a_rk, a_lnx_w, a_lnx_b, b_conv_w, b_conv_b, b_ln_g, b_ln_b, c_pool_w, c_scale, proj_a, proj_b, proj_c, w_out, ffn_norm_g, ffn_gate, ffn_up, ffn_down, router_w, router_b, exp_gate, exp_up, exp_down, final_norm_g):
    bn, seq, d = x.shape
    depth = w_in.shape[0]
    L = N_META + seq
    assert 2 * bn * A_HEADS == V7X_LANES, "the recurrence kernel maps (value half, batch, head) onto the 128 lanes"
    bf = lambda w: w.astype(BF16)
    meta = jnp.broadcast_to(meta_tokens[None].astype(x.dtype), (bn, N_META, d))
    h = jnp.concatenate([meta, x], 1).reshape(bn * L, d)
    for i in range(depth):
        p = _layer_params(i, d, mix_norm_g, w_in, gate_b, a_mu, a_w0, a_w2, a_a0, a_a2, a_g2, a_kk, a_ka, a_rk,
                          a_lnx_w, a_lnx_b, b_conv_w, b_conv_b, b_ln_g, b_ln_b, c_pool_w, c_scale,
                          proj_a, proj_b, proj_c, w_out)
        last = i == depth - 1
        j = i // 2
        moe = i % 2 == 1
        if moe:
            p["ffn_g"] = ffn_norm_g[i].reshape(1, d)
            rw = jnp.zeros((d, ROUTE_LANES), F32).at[:, :N_EXPERTS].set(router_w[j])
            p["router_w_hi"] = bf(rw)
            p["router_w_lo"] = bf(rw - bf(rw).astype(F32))
            p["router_b"] = jnp.zeros((1, ROUTE_LANES), F32).at[0, :N_EXPERTS].set(router_b[j])
        na, dec, bb, k2, r, v, bonus, g, y_b, y_c = mixer_prologue(h, bn, L, p)
        y = wkv_natural(na, dec, bb, k2, r, v, bn, L)
        flat = lambda a: a.reshape(bn * L, a.shape[-1])
        merged = mixer_merge(y, flat(bonus), flat(g), flat(y_b), flat(y_c), h, p, moe)
        if moe:
            h, n, logits = merged
            block_e, slot_tok, gates, dest = _route(logits[:, :N_EXPERTS])
            y_slots = moe_expert_ffn(n, block_e, slot_tok, bf(exp_gate[j]), bf(exp_up[j]), bf(exp_down[j]))
            h = moe_combine(h, y_slots, dest, gates, final_norm_g, last)
        else:
            (h,) = merged
            h = dense_ffn(h, ffn_norm_g[i], bf(ffn_gate[j]), bf(ffn_up[j]), bf(ffn_down[j]), final_norm_g, last)
    return h.reshape(bn, L, d)[:, N_META:]
```

```python
import functools

import jax
import jax.numpy as jnp
from jax import lax
from jax.experimental import pallas as pl
from jax.experimental.pallas import tpu as pltpu

F32 = jnp.float32
BF16 = jnp.bfloat16

N_META = 16
A_HEADS = 8
A_HEAD_DIM = 64
A_WIDTH = A_HEADS * A_HEAD_DIM
W_LORA = 64
A_LORA = 64
G_LORA = 128
A_IN = 3 * A_WIDTH + W_LORA + A_LORA + G_LORA
LNX_EPS = 64e-5
B_WIDTH = 512
CONV_WIDTH = 31
C_WIDTH = 512
POOL_WINDOWS = (2, 4, 8, 16)
C_GROUP_DIM = C_WIDTH // len(POOL_WINDOWS)
N_EXPERTS = 8
TOP_K = 2
RMS_EPS = 1e-6
LN_EPS = 1e-5

V7X_LANES = 128
V7X_SUBLANES = 8
VMEM_LIMIT_BYTES = 58 * 1024 * 1024
ROUTE_LANES = V7X_LANES
MOE_ROWS = 512
CONV_HIST = 32
POOL_HIST = 16
CONV_ROWS = 16
CONV_PARTS = 4

R0, K0, V0 = 0, A_WIDTH, 2 * A_WIDTH
WD0 = 3 * A_WIDTH
AD0 = WD0 + V7X_LANES
GD0 = AD0 + V7X_LANES
A_END = GD0 + G_LORA
BL0 = A_END
BG0 = BL0 + B_WIDTH
C0 = BG0 + B_WIDTH
P_END = C0 + C_WIDTH


def _pick(n, candidates):
    for c in candidates:
        if n % c == 0:
            return c
    raise ValueError(f"no tile in {candidates} divides {n}")


def _params(*sem):
    return pltpu.CompilerParams(dimension_semantics=sem, vmem_limit_bytes=VMEM_LIMIT_BYTES)


def _rms(x, g):
    return x * lax.rsqrt(jnp.mean(x * x, -1, keepdims=True) + RMS_EPS) * g


def _silu(x):
    return x * jax.nn.sigmoid(x)


def _softplus(x):
    return jnp.maximum(x, 0.0) + jnp.log(1.0 + jnp.exp(-jnp.abs(x)))


def _bdot(x, w):
    return jnp.dot(x.astype(BF16), w, preferred_element_type=F32)


def _head_sum(x, ones_ref):
    hi = x.astype(BF16)
    lo = (x - hi.astype(F32)).astype(BF16)
    ones = ones_ref[...]
    return jnp.dot(hi, ones, preferred_element_type=F32) + jnp.dot(lo, ones, preferred_element_type=F32)


def _prologue_kernel(h_ref, ng_ref, w_ref, mu_ref, w0_ref, w2_ref, a0_ref, a2_ref, g2_ref, kk_ref, ka_ref, rk_ref,
                     cw_ref, cb_ref, lg_ref, lb_ref, pw_ref, ps_ref, ones_ref,
                     na_ref, dec_ref, bb_ref, k2_ref, r_ref, v_ref, bonus_ref, g_ref, yb_ref, yc_ref,
                     zbuf_ref, cbuf_ref, sbuf_ref, craw_ref, pbuf_ref, *, tt):
    i = pl.program_id(1)

    @pl.when(i == 0)
    def _():
        zbuf_ref[pl.ds(0, V7X_SUBLANES), :] = jnp.zeros((V7X_SUBLANES, A_END), F32)
        cbuf_ref[pl.ds(0, CONV_HIST), :] = jnp.zeros((CONV_HIST, B_WIDTH), F32)
        pbuf_ref[pl.ds(0, POOL_HIST), :] = jnp.zeros((POOL_HIST, C_WIDTH), F32)

    n = _rms(h_ref[0], ng_ref[...]).astype(BF16)

    za = jnp.dot(n, w_ref[:, 0:A_END], preferred_element_type=F32)
    zbuf_ref[pl.ds(V7X_SUBLANES, tt), :] = za
    prev = zbuf_ref[pl.ds(V7X_SUBLANES - 1, tt), :]
    zbuf_ref[pl.ds(V7X_SUBLANES - 1, 1), :] = zbuf_ref[pl.ds(V7X_SUBLANES - 1 + tt, 1), :]
    za = za + (prev - za) * mu_ref[...]
    r = za[:, R0:R0 + A_WIDTH]
    k = za[:, K0:K0 + A_WIDTH]
    v = za[:, V0:V0 + A_WIDTH]
    w_log = -_softplus(-(w0_ref[...] + _bdot(jnp.tanh(za[:, WD0:AD0]), w2_ref[...]))) - 0.5
    dec_ref[0] = jnp.exp(-jnp.exp(w_log))
    a = jax.nn.sigmoid(a0_ref[...] + _bdot(za[:, AD0:GD0], a2_ref[...]))
    g_ref[0] = _bdot(jax.nn.sigmoid(za[:, GD0:A_END]), g2_ref[...])
    kk = k * kk_ref[...]
    kk = kk / jnp.maximum(jnp.sqrt(_head_sum(kk * kk, ones_ref)), 1e-12)
    k2 = k * (1.0 + (a - 1.0) * ka_ref[...])
    na_ref[0] = (-kk).astype(BF16)
    bb_ref[0] = (kk * a).astype(BF16)
    k2_ref[0] = k2.astype(BF16)
    r_ref[0] = r.astype(BF16)
    v_ref[0] = v.astype(BF16)
    bonus_ref[0] = _head_sum(r * k2 * rk_ref[...], ones_ref) * v

    glu = (jnp.dot(n, w_ref[:, BL0:BG0], preferred_element_type=F32)
           * jax.nn.sigmoid(jnp.dot(n, w_ref[:, BG0:C0], preferred_element_type=F32)))
    cbuf_ref[pl.ds(CONV_HIST, tt), :] = glu
    span = tt + CONV_HIST - V7X_SUBLANES
    for s in range(1, V7X_SUBLANES):
        sbuf_ref[s - 1] = cbuf_ref[pl.ds(s, span), :]
    first = CONV_HIST - (CONV_WIDTH - 1)

    def conv_rows(c, carry):
        base = pl.multiple_of(c * CONV_ROWS, CONV_ROWS)
        parts = [None] * CONV_PARTS
        for j in range(CONV_WIDTH):
            q, s = divmod(first + j, V7X_SUBLANES)
            rows = pl.ds(base + q * V7X_SUBLANES, CONV_ROWS)
            tap = cbuf_ref[rows, :] if s == 0 else sbuf_ref[s - 1, rows, :]
            term = cw_ref[pl.ds(j, 1), :] * tap
            parts[j % CONV_PARTS] = term if parts[j % CONV_PARTS] is None else parts[j % CONV_PARTS] + term
        craw_ref[pl.ds(base, CONV_ROWS), :] = functools.reduce(lambda p, q: p + q, parts)
        return carry

    lax.fori_loop(0, tt // CONV_ROWS, conv_rows, 0)
    cbuf_ref[pl.ds(0, CONV_HIST), :] = cbuf_ref[pl.ds(tt, CONV_HIST), :]
    conv = craw_ref[...] + cb_ref[...]
    cen = conv - jnp.mean(conv, -1, keepdims=True)
    var = jnp.mean(cen * cen, -1, keepdims=True)
    yb_ref[0] = _silu(cen * lax.rsqrt(var + LN_EPS) * lg_ref[...] + lb_ref[...]).astype(BF16)

    pbuf_ref[pl.ds(POOL_HIST, tt), :] = jnp.dot(n, w_ref[:, C0:P_END], preferred_element_type=F32)
    pos = (lax.broadcasted_iota(jnp.int32, (tt, C_GROUP_DIM), 0) + (i * tt + 1)).astype(F32)
    for gi, win in enumerate(POOL_WINDOWS):
        lanes = pl.ds(gi * C_GROUP_DIM, C_GROUP_DIM)
        cur = pbuf_ref[pl.ds(POOL_HIST, tt), lanes]
        tot = cur
        for d in range(1, win):
            tot = tot + pbuf_ref[pl.ds(POOL_HIST - d, tt), lanes]
        pooled = tot / jnp.minimum(pos, float(win)) - cur
        yc_ref[0, :, lanes] = (_bdot(pooled, pw_ref[gi]) * ps_ref[:, lanes]).astype(BF16)
    pbuf_ref[pl.ds(0, POOL_HIST), :] = pbuf_ref[pl.ds(tt, POOL_HIST), :]


def mixer_prologue(h, bn, L, p):
    d = h.shape[-1]
    tt = _pick(L, (304, 144, 48))
    assert tt % CONV_ROWS == 0
    const = lambda a: pl.BlockSpec(a.shape, lambda b, i: (0,) * a.ndim)
    consts = [p["norm_g"], p["w_p"], p["mu"], p["w0"], p["w2"], p["a0"], p["a2"], p["g2"], p["k_k"], p["k_a"],
              p["r_k"], p["conv_w"], p["conv_b"], p["ln_g"], p["ln_b"], p["pool_w"], p["pool_s"], p["ones"]]
    out_spec = pl.BlockSpec((1, tt, A_WIDTH), lambda b, i: (b, i, 0))
    out_dtypes = [BF16, F32, BF16, BF16, BF16, BF16, F32, F32, BF16, BF16]
    return pl.pallas_call(
        functools.partial(_prologue_kernel, tt=tt),
        grid=(bn, L // tt),
        in_specs=[pl.BlockSpec((1, tt, d), lambda b, i: (b, i, 0))] + [const(a) for a in consts],
        out_specs=[out_spec] * len(out_dtypes),
        out_shape=[jax.ShapeDtypeStruct((bn, L, A_WIDTH), dt) for dt in out_dtypes],
        scratch_shapes=[
            pltpu.VMEM((tt + V7X_SUBLANES, A_END), F32),
            pltpu.VMEM((tt + CONV_HIST, B_WIDTH), F32),
            pltpu.VMEM((V7X_SUBLANES - 1, tt + CONV_HIST - V7X_SUBLANES, B_WIDTH), F32),
            pltpu.VMEM((tt, B_WIDTH), F32),
            pltpu.VMEM((tt + POOL_HIST, C_WIDTH), F32),
        ],
        compiler_params=_params("parallel", "arbitrary"),
        name="mixer_prologue",
    )(h.reshape(bn, L, d), *consts)


def _sublane_total(x):
    p = jnp.sum(x, axis=0)
    p = p + pltpu.roll(p, 4, 0)
    p = p + pltpu.roll(p, 2, 0)
    return p + pltpu.roll(p, 1, 0)


def _wkv_kernel(a_ref, w_ref, b_ref, k_ref, r_ref, v_ref, y_ref, s_ref, x_ref, vx_ref, *, tt, nv):
    @pl.when(pl.program_id(0) == 0)
    def _():
        s_ref[...] = jnp.zeros_like(s_ref)

    half = V7X_LANES // 2
    rows = tt * nv
    low_lanes = lax.broadcasted_iota(jnp.int32, (rows, V7X_LANES), 1) < half
    for c, ref in enumerate((a_ref, w_ref, b_ref, k_ref, r_ref)):
        x = ref[...].astype(F32).reshape(rows, V7X_LANES)
        swapped = pltpu.roll(x, half, 1)
        x_ref[c, :, 0:4] = jnp.where(low_lanes, x, swapped).reshape(tt, 4, 8, V7X_LANES)
        x_ref[c, :, 4:8] = jnp.where(low_lanes, swapped, x).reshape(tt, 4, 8, V7X_LANES)
    vx_ref[...] = v_ref[...].astype(F32)

    def step(t, carry):
        a = x_ref[0, t]
        w = x_ref[1, t]
        b = x_ref[2, t]
        k = x_ref[3, t]
        r = x_ref[4, t]
        wr = w * r
        br = _sublane_total(b * r)
        kr = _sublane_total(k * r)
        for vi in range(nv):
            s = s_ref[vi]
            vrow = vx_ref[t, pl.ds(vi, 1), :]
            sa = _sublane_total(s * a)
            y0 = _sublane_total(s * wr)
            s_ref[vi] = s * w + sa[None] * b + vrow[None] * k
            y = y0 + sa * br + vrow * kr
            y_ref[t, pl.ds(vi, 1), :] = y[0:1]
        return carry

    lax.fori_loop(0, tt, step, 0)


def wkv_scan(a_s, w_s, b_s, k_s, r_s, v_s):
    L = a_s.shape[0]
    nv = v_s.shape[1]
    tt = _pick(L, (48, 16, 8, 1))
    row = pl.BlockSpec((tt, nv, V7X_LANES), lambda i: (i, 0, 0))
    return pl.pallas_call(
        functools.partial(_wkv_kernel, tt=tt, nv=nv),
        grid=(L // tt,),
        in_specs=[row] * 6,
        out_specs=row,
        out_shape=jax.ShapeDtypeStruct((L, nv, V7X_LANES), F32),
        scratch_shapes=[pltpu.VMEM((nv, 8, 8, V7X_LANES), F32), pltpu.VMEM((5, tt, 8, 8, V7X_LANES), F32),
                        pltpu.VMEM((tt, nv, V7X_LANES), F32)],
        compiler_params=_params("arbitrary"),
        name="wkv_scan",
    )(a_s, w_s, b_s, k_s, r_s, v_s)


def wkv_natural(na, dec, bb, k2, r, v, bn, L):
    half = A_HEAD_DIM // 2

    def to_lanes(x):
        return x.reshape(bn, L, A_HEADS, 2, half).transpose(1, 4, 3, 0, 2).reshape(L, half, 2 * bn * A_HEADS)

    y_s = wkv_scan(to_lanes(na), to_lanes(dec), to_lanes(bb), to_lanes(k2), to_lanes(r), to_lanes(v))
    return y_s.reshape(L, half, 2, bn, A_HEADS).transpose(3, 0, 4, 2, 1).reshape(bn * L, A_WIDTH)


def _merge_kernel(y_ref, bonus_ref, g_ref, yb_ref, yc_ref, h_ref, ng_ref, wg_ref, gb_ref, lw_ref, lb_ref,
                  pa_ref, pb_ref, pc_ref, wo_ref, ones_ref, *rest, route):
    if route:
        fg_ref, rwh_ref, rwl_ref, rb_ref, o_ref, n_ref, l_ref = rest
    else:
        (o_ref,) = rest
    d = h_ref.shape[-1]
    h = h_ref[...]
    gates = jax.nn.sigmoid(jnp.dot(_rms(h, ng_ref[...]).astype(BF16), wg_ref[...], preferred_element_type=F32)
                           + gb_ref[...])
    y = y_ref[...]
    cen = y - _head_sum(y, ones_ref) * (1.0 / A_HEAD_DIM)
    var = _head_sum(cen * cen, ones_ref) * (1.0 / A_HEAD_DIM)
    ya = (cen * lax.rsqrt(var + LNX_EPS) * lw_ref[...] + lb_ref[...] + bonus_ref[...]) * g_ref[...]
    merged = (gates[:, :d] * _bdot(ya, pa_ref[...]) + gates[:, d:2 * d] * _bdot(yb_ref[...], pb_ref[...])
              + gates[:, 2 * d:] * _bdot(yc_ref[...], pc_ref[...]))
    out = h + _bdot(merged, wo_ref[...])
    o_ref[...] = out
    if route:
        n = _rms(out, fg_ref[...])
        n_ref[...] = n
        n_hi = n.astype(BF16)
        n_lo = (n - n_hi.astype(F32)).astype(BF16)
        dot = lambda a, b_ref: jnp.dot(a, b_ref[...], preferred_element_type=F32)
        l_ref[...] = dot(n_hi, rwh_ref) + dot(n_lo, rwh_ref) + dot(n_hi, rwl_ref) + rb_ref[...]


def mixer_merge(y, bonus, g, y_b, y_c, h, p, route):
    m, d = h.shape
    tm = _pick(m, (384, 128, 8))
    rows = lambda width: pl.BlockSpec((tm, width), lambda i: (i, 0))
    const = lambda a: pl.BlockSpec(a.shape, lambda i: (0,) * a.ndim)
    consts = [p["norm_g"], p["w_g"], p["gate_b"], p["lnx_w"], p["lnx_b"], p["proj_a"], p["proj_b"], p["proj_c"],
              p["w_out"], p["ones"]]
    out_specs = [rows(d)]
    out_shape = [jax.ShapeDtypeStruct((m, d), F32)]
    if route:
        consts += [p["ffn_g"], p["router_w_hi"], p["router_w_lo"], p["router_b"]]
        out_specs += [rows(d), rows(ROUTE_LANES)]
        out_shape += [jax.ShapeDtypeStruct((m, d), F32), jax.ShapeDtypeStruct((m, ROUTE_LANES), F32)]
    return pl.pallas_call(
        functools.partial(_merge_kernel, route=route),
        grid=(m // tm,),
        in_specs=[rows(A_WIDTH)] * 3 + [rows(B_WIDTH), rows(C_WIDTH), rows(d)] + [const(a) for a in consts],
        out_specs=out_specs,
        out_shape=out_shape,
        compiler_params=_params("parallel"),
        name="mixer_merge",
    )(y, bonus, g, y_b, y_c, h, *consts)


def _ffn_kernel(h_ref, g_ref, fg_ref, wg_ref, wu_ref, wd_ref, o_ref, xn_ref, acc_ref, *, final):
    j = pl.program_id(1)

    @pl.when(j == 0)
    def _():
        xn_ref[...] = _rms(h_ref[...], g_ref[...]).astype(BF16)
        acc_ref[...] = jnp.zeros_like(acc_ref)

    x = xn_ref[...]
    gate = jnp.dot(x, wg_ref[...], preferred_element_type=F32)
    up = jnp.dot(x, wu_ref[...], preferred_element_type=F32)
    acc_ref[...] += _bdot(_silu(gate) * up, wd_ref[...])

    @pl.when(j == pl.num_programs(1) - 1)
    def _():
        hh = h_ref[...] + acc_ref[...]
        o_ref[...] = _rms(hh, fg_ref[...]) if final else hh


def dense_ffn(h, g, w_gate, w_up, w_down, final_g, final):
    m, d = h.shape
    ff = w_gate.shape[1]
    tm = _pick(m, (576, 384, 128, 8))
    tf = _pick(ff, (1408, 256, 128))
    return pl.pallas_call(
        functools.partial(_ffn_kernel, final=final),
        grid=(m // tm, ff // tf),
        in_specs=[
            pl.BlockSpec((tm, d), lambda i, j: (i, 0)),
            pl.BlockSpec((1, d), lambda i, j: (0, 0)),
            pl.BlockSpec((1, d), lambda i, j: (0, 0)),
            pl.BlockSpec((d, tf), lambda i, j: (0, j)),
            pl.BlockSpec((d, tf), lambda i, j: (0, j)),
            pl.BlockSpec((tf, d), lambda i, j: (j, 0)),
        ],
        out_specs=pl.BlockSpec((tm, d), lambda i, j: (i, 0)),
        out_shape=jax.ShapeDtypeStruct((m, d), F32),
        scratch_shapes=[pltpu.VMEM((tm, d), BF16), pltpu.VMEM((tm, d), F32)],
        compiler_params=_params("parallel", "arbitrary"),
        name="dense_ffn",
    )(h, g.reshape(1, d), final_g.reshape(1, d), w_gate, w_up, w_down)


def _row_copy(idx_ref, src_hbm, dst_ref, sem, r):
    return pltpu.make_async_copy(src_hbm.at[pl.ds(idx_ref[0, 0, r], 1)], dst_ref.at[pl.ds(r, 1)], sem)


def _start_row_gather(idx_ref, src_hbm, dst_ref, sem, n_rows):
    def start(r, c):
        _row_copy(idx_ref, src_hbm, dst_ref, sem, r).start()
        return c

    lax.fori_loop(0, n_rows, start, 0, unroll=8)


def _wait_row_gather(idx_ref, src_hbm, dst_ref, sem, n_rows):
    def wait(r, c):
        _row_copy(idx_ref, src_hbm, dst_ref, sem, r).wait()
        return c

    lax.fori_loop(0, n_rows, wait, 0, unroll=8)


def _moe_ffn_kernel(be_ref, tok_ref, nxt_ref, x_hbm, wg_ref, wu_ref, wd_ref, o_ref,
                    xraw_ref, x_ref, acc_ref, sem, *, rows_per_step):
    del be_ref
    i = pl.program_id(0)
    j = pl.program_id(1)
    slot = i % 2
    last_j = pl.num_programs(1) - 1

    @pl.when(j == 0)
    def _():
        @pl.when(i == 0)
        def _():
            _start_row_gather(tok_ref, x_hbm, xraw_ref.at[0], sem.at[0], MOE_ROWS)

        _wait_row_gather(tok_ref, x_hbm, xraw_ref.at[slot], sem.at[slot], MOE_ROWS)
        x_ref[...] = xraw_ref[slot].astype(BF16)
        acc_ref[...] = jnp.zeros_like(acc_ref)

    nxt_buf = xraw_ref.at[1 - slot]
    nxt_sem = sem.at[1 - slot]
    for r in range(rows_per_step):
        _row_copy(nxt_ref, x_hbm, nxt_buf, nxt_sem, j * rows_per_step + r).start()

    x = x_ref[...]
    gate = jnp.dot(x, wg_ref[0], preferred_element_type=F32)
    up = jnp.dot(x, wu_ref[0], preferred_element_type=F32)
    acc_ref[...] += _bdot(_silu(gate) * up, wd_ref[0])

    @pl.when(j == last_j)
    def _():
        o_ref[...] = acc_ref[...]

    @pl.when((i == pl.num_programs(0) - 1) & (j == last_j))
    def _():
        _wait_row_gather(nxt_ref, x_hbm, nxt_buf, nxt_sem, MOE_ROWS)


def moe_expert_ffn(n, block_e, slot_tok, e_gate, e_up, e_down):
    d = n.shape[1]
    ff = e_gate.shape[2]
    nb = block_e.shape[0]
    tf = _pick(ff, (1408, 256, 128))
    tok = slot_tok.reshape(nb, 1, MOE_ROWS)
    assert MOE_ROWS % (ff // tf) == 0
    grid_spec = pltpu.PrefetchScalarGridSpec(
        num_scalar_prefetch=1,
        grid=(nb, ff // tf),
        in_specs=[
            pl.BlockSpec((1, 1, MOE_ROWS), lambda i, j, be: (i, 0, 0), memory_space=pltpu.SMEM),
            pl.BlockSpec((1, 1, MOE_ROWS), lambda i, j, be: (jnp.minimum(i + 1, nb - 1), 0, 0),
                         memory_space=pltpu.SMEM),
            pl.BlockSpec(memory_space=pl.ANY),
            pl.BlockSpec((1, d, tf), lambda i, j, be: (be[i], 0, j)),
            pl.BlockSpec((1, d, tf), lambda i, j, be: (be[i], 0, j)),
            pl.BlockSpec((1, tf, d), lambda i, j, be: (be[i], j, 0)),
        ],
        out_specs=pl.BlockSpec((MOE_ROWS, d), lambda i, j, be: (i, 0)),
        scratch_shapes=[pltpu.VMEM((2, MOE_ROWS, d), F32), pltpu.VMEM((MOE_ROWS, d), BF16),
                        pltpu.VMEM((MOE_ROWS, d), F32), pltpu.SemaphoreType.DMA((2,))],
    )
    return pl.pallas_call(
        functools.partial(_moe_ffn_kernel, rows_per_step=MOE_ROWS // (ff // tf)),
        grid_spec=grid_spec,
        out_shape=jax.ShapeDtypeStruct((nb * MOE_ROWS, d), F32),
        compiler_params=_params("arbitrary", "arbitrary"),
        name="moe_expert_ffn",
    )(block_e, tok, tok, n, e_gate, e_up, e_down)


def _combine_kernel(d_ref, h_ref, gate_ref, g_ref, y_hbm, o_ref, buf_ref, sem, *, tm, final):
    for r in range(TOP_K * tm):
        _row_copy(d_ref, y_hbm, buf_ref, sem, r).start()
    _wait_row_gather(d_ref, y_hbm, buf_ref, sem, TOP_K * tm)
    hh = h_ref[...]
    for k in range(TOP_K):
        hh = hh + buf_ref[pl.ds(k * tm, tm), :] * gate_ref[:, k:k + 1]
    o_ref[...] = _rms(hh, g_ref[...]) if final else hh


def moe_combine(h, y_slots, dest, gates, g, final):
    m, d = h.shape
    tm = _pick(m, (384, 128, 8))
    nt = m // tm
    idx = dest.reshape(nt, tm, TOP_K).transpose(0, 2, 1).reshape(nt, 1, TOP_K * tm)
    return pl.pallas_call(
        functools.partial(_combine_kernel, tm=tm, final=final),
        grid=(nt,),
        in_specs=[
            pl.BlockSpec((1, 1, TOP_K * tm), lambda i: (i, 0, 0), memory_space=pltpu.SMEM),
            pl.BlockSpec((tm, d), lambda i: (i, 0)),
            pl.BlockSpec((tm, TOP_K), lambda i: (i, 0)),
            pl.BlockSpec((1, d), lambda i: (0, 0)),
            pl.BlockSpec(memory_space=pl.ANY),
        ],
        out_specs=pl.BlockSpec((tm, d), lambda i: (i, 0)),
        out_shape=jax.ShapeDtypeStruct((m, d), F32),
        scratch_shapes=[pltpu.VMEM((TOP_K * tm, d), F32), pltpu.SemaphoreType.DMA(())],
        compiler_params=_params("arbitrary"),
        name="moe_combine",
    )(idx, h, gates, g.reshape(1, d), y_slots)


def _route(logits):
    t = logits.shape[0]
    top_l, top_e = lax.top_k(logits, TOP_K)
    gates = jax.nn.softmax(top_l, -1)
    flat_e = top_e.reshape(-1).astype(jnp.int32)
    onehot = (flat_e[:, None] == jnp.arange(N_EXPERTS, dtype=jnp.int32)[None, :]).astype(jnp.int32)
    csum = jnp.cumsum(onehot, axis=0)
    rank = jnp.sum(csum * onehot, axis=1) - 1
    counts = csum[-1]
    padded = (counts + MOE_ROWS - 1) // MOE_ROWS * MOE_ROWS
    pad_end = jnp.cumsum(padded)
    dest = (pad_end - padded)[flat_e] + rank
    nb = (t * TOP_K + N_EXPERTS * (MOE_ROWS - 1) + MOE_ROWS - 1) // MOE_ROWS
    slot_tok = jnp.zeros((nb * MOE_ROWS,), jnp.int32).at[dest].set(jnp.arange(t * TOP_K, dtype=jnp.int32) // TOP_K)
    block_e = jnp.minimum(jnp.searchsorted(pad_end, jnp.arange(nb, dtype=jnp.int32) * MOE_ROWS, side='right'),
                          N_EXPERTS - 1).astype(jnp.int32)
    return block_e, slot_tok, gates, dest.reshape(t, TOP_K).astype(jnp.int32)


def _layer_params(i, d, mix_norm_g, w_in, gate_b, a_mu, a_w0, a_w2, a_a0, a_a2, a_g2, a_kk, a_ka, a_rk, a_lnx_w,
                  a_lnx_b, b_conv_w, b_conv_b, b_ln_g, b_ln_b, c_pool_w, c_scale, proj_a, proj_b, proj_c, w_out):
    bf = lambda w: w.astype(BF16)
    row = lambda a: a.reshape(1, -1)
    s_a, s_b, s_c = A_IN, A_IN + 2 * B_WIDTH, A_IN + 2 * B_WIDTH + C_WIDTH

    def lane_pad(a, axis):
        o = 3 * A_WIDTH
        take = lambda lo, hi: lax.slice_in_dim(a, lo, hi, axis=axis)
        zshape = list(a.shape)
        zshape[axis] = V7X_LANES - W_LORA
        z = jnp.zeros(zshape, a.dtype)
        return jnp.concatenate([take(0, o), take(o, o + W_LORA), z, take(o + W_LORA, o + W_LORA + A_LORA), z,
                                take(o + W_LORA + A_LORA, s_a)], axis)

    w = w_in[i]
    w_p = jnp.concatenate([lane_pad(w[:, :s_a], 1), w[:, s_a:s_c]], 1)
    pad_rows = lambda a: jnp.concatenate([a, jnp.zeros((V7X_LANES - a.shape[0], a.shape[1]), a.dtype)], 0)
    head = jnp.arange(A_WIDTH, dtype=jnp.int32) // A_HEAD_DIM
    return {
        "norm_g": row(mix_norm_g[i]), "w_p": bf(w_p), "mu": row(lane_pad(a_mu[i], 0)),
        "w0": row(a_w0[i]), "w2": bf(pad_rows(a_w2[i])), "a0": row(a_a0[i]), "a2": bf(pad_rows(a_a2[i])),
        "g2": bf(a_g2[i]), "k_k": row(a_kk[i]), "k_a": row(a_ka[i]), "r_k": row(a_rk[i]),
        "conv_w": b_conv_w[i].reshape(CONV_WIDTH, B_WIDTH), "conv_b": row(b_conv_b[i]),
        "ln_g": row(b_ln_g[i]), "ln_b": row(b_ln_b[i]), "pool_w": bf(c_pool_w[i]), "pool_s": row(c_scale[i]),
        "ones": (head[:, None] == head[None, :]).astype(BF16),
        "w_g": bf(w[:, s_c:]), "gate_b": row(gate_b[i]), "lnx_w": row(a_lnx_w[i]), "lnx_b": row(a_lnx_b[i]),
        "proj_a": bf(proj_a[i]), "proj_b": bf(proj_b[i]), "proj_c": bf(proj_c[i]), "w_out": bf(w_out[i]),
    }


def kernel(x, meta_tokens, mix_norm_g, w_in, gate_b, a_mu, a_w0, a_w2, a_a0, a_a2, a_g2, a_kk, a_ka, a_rk, a_lnx_w, a_lnx_b, b_conv_w, b_conv_b, b_ln_g, b_ln_b, c_pool_w, c_scale, proj_a, proj_b, proj_c, w_out, ffn_norm_g, ffn_gate, ffn_up, ffn_down, router_w, router_b, exp_gate, exp_up, exp_down, final_norm_g):
    bn, seq, d = x.shape
    depth = w_in.shape[0]
    L = N_META + seq
    assert 2 * bn * A_HEADS == V7X_LANES, "the recurrence kernel maps (value half, batch, head) onto the 128 lanes"
    bf = lambda w: w.astype(BF16)
    meta = jnp.broadcast_to(meta_tokens[None].astype(x.dtype), (bn, N_META, d))
    h = jnp.concatenate([meta, x], 1).reshape(bn * L, d)
    for i in range(depth):
        p = _layer_params(i, d, mix_norm_g, w_in, gate_b, a_mu, a_w0, a_w2, a_a0, a_a2, a_g2, a_kk, a_ka, a_rk,
                          a_lnx_w, a_lnx_b, b_conv_w, b_conv_b, b_ln_g, b_ln_b, c_pool_w, c_scale,
                          proj_a, proj_b, proj_c, w_out)
        last = i == depth - 1
        j = i // 2
        moe = i % 2 == 1
        if moe:
            p["ffn_g"] = ffn_norm_g[i].reshape(1, d)
            rw = jnp.zeros((d, ROUTE_LANES), F32).at[:, :N_EXPERTS].set(router_w[j])
            p["router_w_hi"] = bf(rw)
            p["router_w_lo"] = bf(rw - bf(rw).astype(F32))
            p["router_b"] = jnp.zeros((1, ROUTE_LANES), F32).at[0, :N_EXPERTS].set(router_b[j])
        na, dec, bb, k2, r, v, bonus, g, y_b, y_c = mixer_prologue(h, bn, L, p)
        y = wkv_natural(na, dec, bb, k2, r, v, bn, L)
        flat = lambda a: a.reshape(bn * L, a.shape[-1])
        merged = mixer_merge(y, flat(bonus), flat(g), flat(y_b), flat(y_c), h, p, moe)
        if moe:
            h, n, logits = merged
            block_e, slot_tok, gates, dest = _route(logits[:, :N_EXPERTS])
            y_slots = moe_expert_ffn(n, block_e, slot_tok, bf(exp_gate[j]), bf(exp_up[j]), bf(exp_down[j]))
            h = moe_combine(h, y_slots, dest, gates, final_norm_g, last)
        else:
            (h,) = merged
            h = dense_ffn(h, ffn_norm_g[i], bf(ffn_gate[j]), bf(ffn_up[j]), bf(ffn_down[j]), final_norm_g, last)
    return h.reshape(bn, L, d)[:, N_META:]
```
